```python
import jax
import jax.numpy as jnp
from jax import lax
import numpy as np

D_MODEL = 1024
BATCH = 32
SEQ = 2048
DEPTH = 4

GRID_W = 64
ROPE_THETA = 10000.0
NORM_EPS = 1e-6

A_HEADS = 8
A_KV_HEADS = 2
A_HEAD_DIM = D_MODEL // 16
A_WIDTH = A_HEADS * A_HEAD_DIM
A_KV_WIDTH = A_KV_HEADS * A_HEAD_DIM
Q_BLOCK = 128

B_HEADS = 8
B_HEAD_DIM = D_MODEL // 16
B_WIDTH = B_HEADS * B_HEAD_DIM
DECAY_LORA = 64
ICLR_LORA = 64
LNX_EPS = 64e-5

C_HEADS = 4
C_QK_HEAD_DIM = D_MODEL // C_HEADS
C_V_HEAD_DIM = 2 * D_MODEL // C_HEADS
C_QK_WIDTH = C_HEADS * C_QK_HEAD_DIM
C_V_WIDTH = C_HEADS * C_V_HEAD_DIM
RET_CHUNK = 128
GN_EPS = 1e-5

A_SPLITS = (A_WIDTH, A_KV_WIDTH, A_KV_WIDTH, A_WIDTH)
B_SHIFT_SPLITS = (B_WIDTH, B_WIDTH, B_WIDTH, DECAY_LORA, DECAY_LORA, ICLR_LORA, ICLR_LORA)
A_IN = sum(A_SPLITS)
B_SHIFT_IN = sum(B_SHIFT_SPLITS)
EVEN_IN = A_IN + B_SHIFT_IN + B_WIDTH
EVEN_MIX = A_WIDTH + B_WIDTH
ODD_SPLITS = (C_QK_WIDTH, C_QK_WIDTH, C_V_WIDTH, C_V_WIDTH)
ODD_IN = sum(ODD_SPLITS)
N_EVEN = (DEPTH + 1) // 2
N_ODD = DEPTH // 2

kernel_name = 'hybrid_gqa_rwkv7_retention_encoder'


def split_cols(x, sizes):
    cuts = [int(c) for c in np.cumsum(sizes)[:-1]]
    return jnp.split(x, cuts, axis=-1)


def rms_norm(x, g, eps=NORM_EPS):
    xf = x.astype(jnp.float32)
    y = xf * lax.rsqrt(jnp.mean(xf * xf, axis=-1, keepdims=True) + eps)
    return (y * g.astype(jnp.float32)).astype(x.dtype)


def head_norm(y, g, eps):
    yf = y.astype(jnp.float32)
    mean = jnp.mean(yf, axis=-1, keepdims=True)
    var = jnp.mean(jnp.square(yf - mean), axis=-1, keepdims=True)
    out = (yf - mean) * lax.rsqrt(var + eps)
    return out.reshape(y.shape[0], y.shape[1], -1) * g.astype(jnp.float32)


def axial_angles(T, dim):
    rows_count = T // GRID_W
    row = jnp.repeat(jnp.arange(rows_count, dtype=jnp.float32), GRID_W)
    col = jnp.tile(jnp.arange(GRID_W, dtype=jnp.float32), rows_count)
    half = dim // 2
    inv_freq = ROPE_THETA ** (-jnp.arange(0, half, 2, dtype=jnp.float32) / half)
    return row[:, None] * inv_freq, col[:, None] * inv_freq


def rotate_half_rope(x, ang):
    x1, x2 = jnp.split(x, 2, axis=-1)
    c = jnp.cos(ang)[None, :, None, :].astype(x.dtype)
    s = jnp.sin(ang)[None, :, None, :].astype(x.dtype)
    return jnp.concatenate([x1 * c - x2 * s, x2 * c + x1 * s], axis=-1)


def axial_rope(x):
    T, dim = x.shape[1], x.shape[-1]
    ang_row, ang_col = axial_angles(T, dim)
    half = dim // 2
    return jnp.concatenate([rotate_half_rope(x[..., :half], ang_row),
                            rotate_half_rope(x[..., half:], ang_col)], axis=-1)


def gqa_attention(q, k, v):
    B, T, Hq, d = q.shape
    G = k.shape[2]
    R = Hq // G
    nb = T // Q_BLOCK
    qb = q.reshape(B, nb, Q_BLOCK, G, R, d).transpose(1, 0, 2, 3, 4, 5)
    scale = d ** -0.5

    def block(qi):
        s = jnp.einsum('bqgrd,bkgd->bgrqk', qi, k).astype(jnp.float32) * scale
        p = jax.nn.softmax(s, axis=-1).astype(v.dtype)
        return jnp.einsum('bgrqk,bkgd->bqgrd', p, v)

    o = lax.map(block, qb)
    return o.transpose(1, 0, 2, 3, 4, 5).reshape(B, T, Hq * d)


def centred_shift(u, mu):
    prev = jnp.pad(u[:, :-1], ((0, 0), (1, 0), (0, 0)))
    nxt = jnp.pad(u[:, 1:], ((0, 0), (0, 1), (0, 0)))
    return u + mu * (0.5 * (prev + nxt) - u)


def rwkv7_scan(r, w, k, v, kk, a):
    B, T, H, N = r.shape

    def step(S, inp):
        r_t, w_t, k_t, v_t, kk_t, a_t = inp
        sa = jnp.einsum('bhij,bhj->bhi', S, -kk_t)
        S = (S * w_t[:, :, None, :] + sa[..., None] * (kk_t * a_t)[:, :, None, :]
             + v_t[..., None] * k_t[:, :, None, :])
        return S, jnp.einsum('bhij,bhj->bhi', S, r_t)

    S0 = jnp.zeros((B, H, N, N), jnp.float32)
    xs = tuple(jnp.swapaxes(t, 0, 1) for t in (r, w, k, v, kk, a))
    _, y = lax.scan(step, S0, xs)
    return jnp.swapaxes(y, 0, 1)


def rwkv7_direction(r, k_raw, v, kk, w_lo, a_lo, w0, w2, a0, a2, k_a, reverse):
    f = lambda t: t.astype(jnp.float32)
    B, T, _ = r.shape
    w_log = -jax.nn.softplus(-(f(w0) + jnp.tanh(f(w_lo)) @ f(w2))) - 0.5
    decay = jnp.exp(-jnp.exp(w_log))
    a = jax.nn.sigmoid(f(a0) + f(a_lo) @ f(a2))
    k = f(k_raw) * (1.0 + (a - 1.0) * f(k_a))
    heads = lambda t: t.reshape(B, T, B_HEADS, B_HEAD_DIM)
    seqs = [heads(f(r)), heads(decay), heads(k), heads(f(v)), kk, heads(a)]
    if reverse:
        seqs = [jnp.flip(t, axis=1) for t in seqs]
    y = rwkv7_scan(*seqs)
    return jnp.flip(y, axis=1) if reverse else y


def retention_direction(q, k, v, log_gamma, strict):
    B, T, H, dk = q.shape
    dv = v.shape[-1]
    C = RET_CHUNK
    n = T // C
    idx = jnp.arange(C, dtype=jnp.float32)
    diff = idx[:, None] - idx[None, :]
    mask = (diff > 0) if strict else (diff >= 0)
    d_intra = jnp.where(mask[None], jnp.exp(jnp.where(mask, diff, 0.0)[None] * log_gamma[:, None, None]), 0.0)
    q_decay = jnp.exp((idx + 1.0)[:, None] * log_gamma[None, :])
    k_decay = jnp.exp((C - 1.0 - idx)[:, None] * log_gamma[None, :])
    chunk_decay = jnp.exp(C * log_gamma)
    qc = jnp.swapaxes(q.reshape(B, n, C, H, dk), 0, 1)
    kc = jnp.swapaxes(k.reshape(B, n, C, H, dk), 0, 1)
    vc = jnp.swapaxes(v.reshape(B, n, C, H, dv), 0, 1)

    def step(Rs, inp):
        qi, ki, vi = inp
        s = jnp.einsum('bchd,blhd->bhcl', qi, ki) * d_intra[None]
        intra = jnp.einsum('bhcl,blhe->bche', s, vi)
        cross = jnp.einsum('bchd,bhde->bche', qi, Rs) * q_decay[None, :, :, None]
        Rs = (Rs * chunk_decay[None, :, None, None]
              + jnp.einsum('blhd,blhe->bhde', ki * k_decay[None, :, :, None], vi))
        return Rs, intra + cross

    R0 = jnp.zeros((B, H, dk, dv), jnp.float32)
    _, o = lax.scan(step, R0, (qc, kc, vc))
    return jnp.swapaxes(o, 0, 1).reshape(B, T, H, dv)


def even_mixer(h, w_in, mu, q_gain, k_gain, k_k, k_a, r_k, w0_f, w2_f, a0_f, a2_f,
               w0_b, w2_b, a0_b, a2_b, lnx_g, lnx_b, w_out):
    B, T, _ = h.shape
    proj = h @ w_in
    a_part, b_shift, b_gate = split_cols(proj, (A_IN, B_SHIFT_IN, B_WIDTH))
    a_q, a_k, a_v, a_g = split_cols(a_part, A_SPLITS)
    q = rms_norm(a_q.reshape(B, T, A_HEADS, A_HEAD_DIM), q_gain)
    k = rms_norm(a_k.reshape(B, T, A_KV_HEADS, A_HEAD_DIM), k_gain)
    v = a_v.reshape(B, T, A_KV_HEADS, A_HEAD_DIM)
    out_a = gqa_attention(axial_rope(q), axial_rope(k), v) * jax.nn.silu(a_g)
    u = centred_shift(b_shift, mu)
    r, kb, vb, w_lo_f, w_lo_b, a_lo_f, a_lo_b = split_cols(u, B_SHIFT_SPLITS)
    heads = lambda t: t.astype(jnp.float32).reshape(B, T, B_HEADS, B_HEAD_DIM)
    kk = heads(kb * k_k)
    kk = kk / jnp.maximum(jnp.sqrt(jnp.sum(kk * kk, axis=-1, keepdims=True)), 1e-12)
    y_f = rwkv7_direction(r, kb, vb, kk, w_lo_f, a_lo_f, w0_f, w2_f, a0_f, a2_f, k_a, False)
    y_b = rwkv7_direction(r, kb, vb, kk, w_lo_b, a_lo_b, w0_b, w2_b, a0_b, a2_b, k_a, True)
    y = head_norm(y_f + y_b, lnx_g, LNX_EPS) + lnx_b.astype(jnp.float32)
    rh, kh, vh = heads(r), heads(kb), heads(vb)
    bonus = (jnp.sum(rh * kh * r_k.astype(jnp.float32), axis=-1, keepdims=True) * vh).reshape(B, T, B_WIDTH)
    out_b = ((y + bonus) * jax.nn.silu(b_gate.astype(jnp.float32))).astype(h.dtype)
    return jnp.concatenate([out_a.astype(h.dtype), out_b], axis=-1) @ w_out


def odd_mixer(h, w_in, gn_g, w_out):
    B, T, _ = h.shape
    q, k, v, g = split_cols(h @ w_in, ODD_SPLITS)
    q = axial_rope(q.reshape(B, T, C_HEADS, C_QK_HEAD_DIM)).astype(jnp.float32)
    k = axial_rope(k.reshape(B, T, C_HEADS, C_QK_HEAD_DIM)).astype(jnp.float32) * (C_QK_HEAD_DIM ** -0.5)
    v = v.reshape(B, T, C_HEADS, C_V_HEAD_DIM).astype(jnp.float32)
    log_gamma_fwd = jnp.log(1.0 - 2.0 ** (-5.0 - jnp.arange(C_HEADS, dtype=jnp.float32)))
    log_gamma_bwd = log_gamma_fwd[::-1]
    o_f = retention_direction(q, k, v, log_gamma_fwd, False)
    o_b = jnp.flip(retention_direction(jnp.flip(q, 1), jnp.flip(k, 1), jnp.flip(v, 1), log_gamma_bwd, True), 1)
    y = head_norm(o_f + o_b, gn_g, GN_EPS)
    return (jax.nn.silu(g.astype(jnp.float32)) * y).astype(h.dtype) @ w_out


def setup_inputs(seed: int = 0) -> dict:
    key = jax.random.key(seed)
    ks = jax.random.split(key, 24)
    nrm = lambda k, shape, scale: scale * jax.random.normal(k, shape, jnp.float32)
    gain = lambda k, shape: 1.0 + 0.02 * jax.random.normal(k, shape, jnp.float32)
    uni = lambda k, shape, lo, hi: jax.random.uniform(k, shape, jnp.float32, lo, hi)
    return {
        'x': nrm(ks[0], (BATCH, SEQ, D_MODEL), 1.0),
        'pre_gain': gain(ks[1], (DEPTH, D_MODEL)),
        'post_gain': gain(ks[2], (DEPTH, D_MODEL)),
        'even_w_in': nrm(ks[3], (N_EVEN, D_MODEL, EVEN_IN), D_MODEL ** -0.5),
        'even_mu': uni(ks[4], (N_EVEN, B_SHIFT_IN), 0.0, 1.0),
        'even_q_gain': gain(ks[5], (N_EVEN, A_HEAD_DIM)),
        'even_k_gain': gain(ks[6], (N_EVEN, A_HEAD_DIM)),
        'even_k_k': 0.85 + 0.02 * jax.random.normal(ks[7], (N_EVEN, B_WIDTH), jnp.float32),
        'even_k_a': gain(ks[8], (N_EVEN, B_WIDTH)),
        'even_r_k': nrm(ks[9], (N_EVEN, B_HEADS, B_HEAD_DIM), 0.1),
        'even_w0_f': uni(ks[10], (N_EVEN, B_WIDTH), -6.0, 1.0),
        'even_w2_f': nrm(ks[11], (N_EVEN, DECAY_LORA, B_WIDTH), 0.5 * DECAY_LORA ** -0.5),
        'even_a0_f': nrm(ks[12], (N_EVEN, B_WIDTH), 0.1),
        'even_a2_f': nrm(ks[13], (N_EVEN, ICLR_LORA, B_WIDTH), 0.5 * ICLR_LORA ** -0.5),
        'even_w0_b': uni(ks[14], (N_EVEN, B_WIDTH), -6.0, 1.0),
        'even_w2_b': nrm(ks[15], (N_EVEN, DECAY_LORA, B_WIDTH), 0.5 * DECAY_LORA ** -0.5),
        'even_a0_b': nrm(ks[16], (N_EVEN, B_WIDTH), 0.1),
        'even_a2_b': nrm(ks[17], (N_EVEN, ICLR_LORA, B_WIDTH), 0.5 * ICLR_LORA ** -0.5),
        'even_lnx_g': gain(ks[18], (N_EVEN, B_WIDTH)),
        'even_lnx_b': nrm(ks[19], (N_EVEN, B_WIDTH), 0.02),
        'even_w_out': nrm(ks[20], (N_EVEN, EVEN_MIX, D_MODEL), EVEN_MIX ** -0.5),
        'odd_w_in': nrm(ks[21], (N_ODD, D_MODEL, ODD_IN), D_MODEL ** -0.5),
        'odd_gn_g': gain(ks[22], (N_ODD, C_V_WIDTH)),
        'odd_w_out': nrm(ks[23], (N_ODD, C_V_WIDTH, D_MODEL), C_V_WIDTH ** -0.5),
    }


def reference(x, pre_gain, post_gain, even_w_in, even_mu, even_q_gain, even_k_gain, even_k_k,
              even_k_a, even_r_k, even_w0_f, even_w2_f, even_a0_f, even_a2_f, even_w0_b,
              even_w2_b, even_a0_b, even_a2_b, even_lnx_g, even_lnx_b, even_w_out,
              odd_w_in, odd_gn_g, odd_w_out):
    h = x
    for layer in range(DEPTH):
        hn = rms_norm(h, pre_gain[layer])
        if layer % 2 == 0:
            i = layer // 2
            m = even_mixer(hn, even_w_in[i], even_mu[i], even_q_gain[i], even_k_gain[i],
                           even_k_k[i], even_k_a[i], even_r_k[i], even_w0_f[i], even_w2_f[i],
                           even_a0_f[i], even_a2_f[i], even_w0_b[i], even_w2_b[i], even_a0_b[i],
                           even_a2_b[i], even_lnx_g[i], even_lnx_b[i], even_w_out[i])
        else:
            j = layer // 2
            m = odd_mixer(hn, odd_w_in[j], odd_gn_g[j], odd_w_out[j])
        h = h + rms_norm(m, post_gain[layer])
    return h
```

```python
import functools

import numpy as np
import jax
import jax.numpy as jnp
from jax import lax
from jax.experimental import pallas as pl
from jax.experimental.pallas import tpu as pltpu

F32 = jnp.float32
BF16 = jnp.bfloat16

GRID_W = 64
ROPE_THETA = 10000.0
NORM_EPS = 1e-6

A_HEADS = 8
A_KV_HEADS = 2
HEAD_DIM = 64
A_WIDTH = A_HEADS * HEAD_DIM
A_KV_WIDTH = A_KV_HEADS * HEAD_DIM
B_HEADS = 8
B_WIDTH = B_HEADS * HEAD_DIM
LORA = 64
LNX_EPS = 64e-5
C_HEADS = 4
GN_EPS = 1e-5

LANES = 128
RWKV_CHUNK = 64
RWKV_GROUP = 8
RET_CHUNK = 256
VMEM_LIMIT = 56 * 1024 * 1024


def _cparams(sem):
    return pltpu.CompilerParams(dimension_semantics=sem, vmem_limit_bytes=VMEM_LIMIT)


def _dot(a, b):
    return jnp.dot(a, b, preferred_element_type=F32)


def _dot_nt(a, b):
    return lax.dot_general(a, b, (((1,), (1,)), ((), ())), preferred_element_type=F32)


def _dot_tn(a, b):
    return lax.dot_general(a, b, (((0,), (0,)), ((), ())), preferred_element_type=F32)


def _bmm(a, b):
    return lax.dot_general(a, b, (((2,), (1,)), ((0,), (0,))), preferred_element_type=F32)


def _bmm_nt(a, b):
    return lax.dot_general(a, b, (((2,), (2,)), ((0,), (0,))), preferred_element_type=F32)


def _bmm_tn(a, b):
    return lax.dot_general(a, b, (((1,), (1,)), ((0,), (0,))), preferred_element_type=F32)


def _sigmoid(x):
    return 1.0 / (1.0 + jnp.exp(-x))


def _silu(x):
    return x * _sigmoid(x)


def _rope_tables(T, dim):
    half = dim // 2
    t = np.arange(T)
    row = (t // GRID_W).astype(np.float32)
    col = (t % GRID_W).astype(np.float32)
    inv_freq = (ROPE_THETA ** (-np.arange(0, half, 2, dtype=np.float32) / half)).astype(np.float32)
    ar = row[:, None] * inv_freq
    ac = col[:, None] * inv_freq
    cos = np.concatenate([np.cos(ar), np.cos(ar), np.cos(ac), np.cos(ac)], axis=1)
    sin = np.concatenate([-np.sin(ar), np.sin(ar), -np.sin(ac), np.sin(ac)], axis=1)
    return jnp.asarray(cos, F32), jnp.asarray(sin, F32)


def _proj_kernel(x_ref, g_ref, w_ref, *rest, col_chunk, rope_cols, k_scale):
    if rope_cols:
        cos_ref, sin_ref, o_ref = rest
    else:
        (o_ref,) = rest
    x = x_ref[...]
    ms = jnp.mean(x * x, axis=-1, keepdims=True)
    hn = (x * lax.rsqrt(ms + NORM_EPS) * g_ref[...]).astype(BF16)
    n_out = o_ref.shape[-1]
    for c0 in range(0, n_out, col_chunk):
        acc = _dot(hn, w_ref[:, c0:c0 + col_chunk])
        if c0 < rope_cols:
            scale = 1.0 if c0 < rope_cols // 2 else k_scale
            for b0 in range(0, col_chunk, LANES):
                xb = acc[:, b0:b0 + LANES]
                t0 = ((c0 + b0) % (2 * LANES))
                cb = cos_ref[:, t0:t0 + LANES]
                sb = sin_ref[:, t0:t0 + LANES]
                yb = xb * cb + pltpu.roll(xb, LANES // 2, 1) * sb
                o_ref[:, c0 + b0:c0 + b0 + LANES] = (yb * scale).astype(o_ref.dtype)
        else:
            o_ref[:, c0:c0 + col_chunk] = acc.astype(o_ref.dtype)


def _proj(h2, gain, w_bf, *, tm, col_chunk, seq, rope=None, k_scale=1.0):
    n_tok, d = h2.shape
    n_out = w_bf.shape[1]
    in_specs = [
        pl.BlockSpec((tm, d), lambda i: (i, 0)),
        pl.BlockSpec((1, d), lambda i: (0, 0)),
        pl.BlockSpec((d, n_out), lambda i: (0, 0)),
    ]
    args = [h2, gain.reshape(1, d), w_bf]
    rope_cols = 0
    if rope is not None:
        cos, sin, rope_cols = rope
        tiles_per_seq = seq // tm
        in_specs += [pl.BlockSpec((tm, cos.shape[1]), lambda i: (i % tiles_per_seq, 0))] * 2
        args += [cos, sin]
    return pl.pallas_call(
        functools.partial(_proj_kernel, col_chunk=col_chunk, rope_cols=rope_cols, k_scale=k_scale),
        grid=(n_tok // tm,),
        in_specs=in_specs,
        out_specs=pl.BlockSpec((tm, n_out), lambda i: (i, 0)),
        out_shape=jax.ShapeDtypeStruct((n_tok, n_out), BF16),
        compiler_params=_cparams(("parallel",)),
        name="proj",
    )(*args)


def _out_kernel(*refs, n_in):
    ins = refs[:n_in]
    w_ref, g_ref, h_ref, o_ref = refs[n_in:]
    acc = None
    r0 = 0
    for m_ref in ins:
        k = m_ref.shape[-1]
        part = _dot(m_ref[...], w_ref[r0:r0 + k, :])
        acc = part if acc is None else acc + part
        r0 += k
    ms = jnp.mean(acc * acc, axis=-1, keepdims=True)
    o_ref[...] = h_ref[...] + acc * lax.rsqrt(ms + NORM_EPS) * g_ref[...]


def _out_proj(mixed, w_bf, gain, h2, *, tm):
    n_tok, d = h2.shape
    k_total = w_bf.shape[0]
    in_specs = [pl.BlockSpec((tm, m.shape[1]), lambda i: (i, 0)) for m in mixed]
    in_specs += [
        pl.BlockSpec((k_total, d), lambda i: (0, 0)),
        pl.BlockSpec((1, d), lambda i: (0, 0)),
        pl.BlockSpec((tm, d), lambda i: (i, 0)),
    ]
    return pl.pallas_call(
        functools.partial(_out_kernel, n_in=len(mixed)),
        grid=(n_tok // tm,),
        in_specs=in_specs,
        out_specs=pl.BlockSpec((tm, d), lambda i: (i, 0)),
        out_shape=jax.ShapeDtypeStruct((n_tok, d), F32),
        compiler_params=_cparams(("parallel",)),
        name="out_proj",
    )(*mixed, w_bf, gain.reshape(1, d), h2)


def _attn_kernel(q_ref, kv_ref, ga_ref, gb_ref, cos_ref, sin_ref, qg_ref, kg_ref, o_ref, kn_ref, *, tq):
    t_idx = pl.program_id(1)
    lane = lax.broadcasted_iota(jnp.int32, (1, LANES), 1)
    head0 = lane < HEAD_DIM
    first = (lane % (HEAD_DIM // 2)) < (HEAD_DIM // 4)

    def norm_rope(x, gain, cos, sin):
        sq = x * x
        s0 = jnp.sum(jnp.where(head0, sq, 0.0), axis=-1, keepdims=True)
        s1 = jnp.sum(jnp.where(head0, 0.0, sq), axis=-1, keepdims=True)
        ms = jnp.where(head0, s0, s1) * (1.0 / HEAD_DIM)
        y = x * lax.rsqrt(ms + NORM_EPS) * gain
        q4 = HEAD_DIM // 4
        partner = jnp.where(first, pltpu.roll(y, LANES - q4, 1), pltpu.roll(y, q4, 1))
        return y * cos + partner * sin

    @pl.when(t_idx == 0)
    def _():
        k = kv_ref[0, :, 0:LANES].astype(F32)
        kn_ref[...] = norm_rope(k, kg_ref[...], cos_ref[...], sin_ref[...]).astype(BF16)

    row0 = pl.multiple_of(t_idx * tq, tq)
    cos_q = cos_ref[pl.ds(row0, tq), :]
    sin_q = sin_ref[pl.ds(row0, tq), :]
    kn = kn_ref[...]
    v = kv_ref[0, :, LANES:2 * LANES]
    scale = HEAD_DIM ** -0.5
    rep = A_HEADS // A_KV_HEADS
    for pair in range(A_HEADS // 2):
        g = (2 * pair) // rep
        qp = q_ref[0, pl.ds(row0, tq), pair * LANES:(pair + 1) * LANES].astype(F32)
        qn = norm_rope(qp, qg_ref[...], cos_q, sin_q) * scale
        qsw = pltpu.roll(qn, HEAD_DIM, 1)
        grp = head0 if g == 0 else jnp.logical_not(head0)
        outs = []
        for sub in range(2):
            src = qn if sub == g else qsw
            qz = jnp.where(grp, src, 0.0).astype(BF16)
            s = _dot_nt(qz, kn)
            m = jnp.max(s, axis=-1, keepdims=True)
            p = jnp.exp(s - m)
            l = jnp.sum(p, axis=-1, keepdims=True)
            o = _dot(p.astype(BF16), v) * (1.0 / l)
            outs.append(o if sub == g else pltpu.roll(o, HEAD_DIM, 1))
        o_pair = jnp.where(head0, outs[0], outs[1])
        gref = ga_ref if pair < 2 else gb_ref
        gc = (pair % 2) * LANES
        gate = gref[0, pl.ds(row0, tq), gc:gc + LANES].astype(F32)
        o_ref[0, :, pair * LANES:(pair + 1) * LANES] = (o_pair * _silu(gate)).astype(o_ref.dtype)


def _attention(proj3, cos, sin, q_gain, k_gain, *, tq):
    B, T, _ = proj3.shape
    qg = jnp.tile(q_gain, 2).reshape(1, LANES)
    kg = jnp.tile(k_gain, 2).reshape(1, LANES)
    return pl.pallas_call(
        functools.partial(_attn_kernel, tq=tq),
        grid=(B, T // tq),
        in_specs=[
            pl.BlockSpec((1, T, A_WIDTH), lambda b, t: (b, 0, 0)),
            pl.BlockSpec((1, T, 2 * A_KV_WIDTH), lambda b, t: (b, 0, A_WIDTH // (2 * A_KV_WIDTH))),
            pl.BlockSpec((1, T, 2 * LANES), lambda b, t: (b, 0, 3)),
            pl.BlockSpec((1, T, 2 * LANES), lambda b, t: (b, 0, 4)),
            pl.BlockSpec((T, LANES), lambda b, t: (0, 0)),
            pl.BlockSpec((T, LANES), lambda b, t: (0, 0)),
            pl.BlockSpec((1, LANES), lambda b, t: (0, 0)),
            pl.BlockSpec((1, LANES), lambda b, t: (0, 0)),
        ],
        out_specs=pl.BlockSpec((1, tq, A_WIDTH), lambda b, t: (b, t, 0)),
        out_shape=jax.ShapeDtypeStruct((B, T, A_WIDTH), BF16),
        scratch_shapes=[pltpu.VMEM((T, LANES), BF16)],
        compiler_params=_cparams(("parallel", "arbitrary")),
        name="attention",
    )(proj3, proj3, proj3, proj3, cos, sin, qg, kg)


def _rwkv_kernel(r_ref, k_ref, v_ref, lo_ref, gate_ref,
                 mur_ref, muk_ref, muv_ref, mulo_ref, wbig_ref,
                 w0f_ref, a0f_ref, w0b_ref, a0b_ref,
                 kk_ref, ka_ref, rk_ref, lng_ref, lnb_ref,
                 o_ref,
                 r_s, v_s, a_s, k_s, b_s, lw_s, y_s,
                 mt_s, gt_s, rp_s, y1_s, dec_s, ss_s):
    T = r_ref.shape[1]
    L = RWKV_CHUNK
    n_chunks = T // L
    CG = min(RWKV_GROUP, n_chunks)
    n_groups = n_chunks // CG
    GT = CG * L
    row = lax.broadcasted_iota(jnp.int32, (T, 1), 0)
    lane = lax.broadcasted_iota(jnp.int32, (1, LANES), 1)
    head0 = lane < HEAD_DIM

    def shift(ref, mu_ref):
        u = ref[0].astype(F32)
        prev = jnp.where(row == 0, 0.0, pltpu.roll(u, 1, 0))
        nxt = jnp.where(row == T - 1, 0.0, pltpu.roll(u, T - 1, 0))
        return u + mu_ref[...] * (0.5 * (prev + nxt) - u)

    def head_sum(x):
        s0 = jnp.sum(jnp.where(head0, x, 0.0), axis=-1, keepdims=True)
        s1 = jnp.sum(jnp.where(head0, 0.0, x), axis=-1, keepdims=True)
        return jnp.where(head0, s0, s1)

    r = shift(r_ref, mur_ref)
    kb = shift(k_ref, muk_ref)
    vb = shift(v_ref, muv_ref)
    lo = shift(lo_ref, mulo_ref)
    lane2 = lax.broadcasted_iota(jnp.int32, (1, 2 * LANES), 1)
    lo_t = jnp.where(lane2 < LANES, jnp.tanh(lo), lo).astype(BF16)
    z = _dot(lo_t, wbig_ref[0])

    kkv = kb * kk_ref[...]
    kkn = kkv / jnp.maximum(jnp.sqrt(head_sum(kkv * kkv)), 1e-12)
    bonus = head_sum(r * kb * rk_ref[...]) * vb

    r_s[...] = r
    v_s[...] = vb
    a_s[...] = -kkn
    for d, (w0_ref, a0_ref) in enumerate(((w0f_ref, a0f_ref), (w0b_ref, a0b_ref))):
        xw = w0_ref[...] + z[:, d * LANES:(d + 1) * LANES]
        softplus = jnp.maximum(-xw, 0.0) + jnp.log(1.0 + jnp.exp(-jnp.abs(xw)))
        lw_s[d] = -jnp.exp(-softplus - 0.5)
        iclr = _sigmoid(a0_ref[...] + z[:, (2 + d) * LANES:(3 + d) * LANES])
        k_s[d] = kb * (1.0 + (iclr - 1.0) * ka_ref[...])
        b_s[d] = kkn * iclr

    pos = lax.broadcasted_iota(jnp.int32, (GT, 1), 0) % L
    ti = lax.broadcasted_iota(jnp.int32, (2 * L, 2 * L), 0)
    si = lax.broadcasted_iota(jnp.int32, (2 * L, 2 * L), 1)
    same = (ti // L) == (si // L)
    eye = (ti == si).astype(F32)

    def stack(x):
        return jnp.concatenate([jnp.where(head0, x, 0.0), jnp.where(head0, 0.0, x)], axis=1)

    for d in range(2):
        reverse = d == 1
        strict = jnp.logical_and(same, (ti < si) if reverse else (ti > si))
        incl = jnp.logical_and(same, (ti <= si) if reverse else (ti >= si))

        def precompute(gi, carry):
            t0 = pl.multiple_of(gi * GT, GT)
            c0 = pl.multiple_of(gi * CG, CG)
            sl = pl.ds(t0, GT)
            lw = lw_s[d, sl, :]
            cf = lw
            sh = 1
            while sh < L:
                cf = cf + jnp.where(pos >= sh, pltpu.roll(cf, sh, 0), 0.0)
                sh *= 2
            to3 = lambda x: x.reshape(CG, L, LANES)
            cf3, lw3 = to3(cf), to3(lw)
            tot = cf3[:, L - 1:L, :]
            if reverse:
                cin = tot - cf3 + lw3
                cex = tot - cf3
            else:
                cin = cf3
                cex = cf3 - lw3
            a3, r3, v3 = to3(a_s[sl, :]), to3(r_s[sl, :]), to3(v_s[sl, :])
            k3, b3 = to3(k_s[d, sl, :]), to3(b_s[d, sl, :])
            ek = jnp.exp(-cin)
            ekt = jnp.exp(tot - cin)
            at = stack(a3 * jnp.exp(cex)).astype(BF16)
            rh = stack(r3 * jnp.exp(cin))
            bh = stack(b3 * ek).astype(BF16)
            kh = stack(k3 * ek).astype(BF16)
            bt = stack(b3 * ekt).astype(BF16)
            kt = stack(k3 * ekt).astype(BF16)
            vs = stack(v3).astype(BF16)
            rh_bf = rh.astype(BF16)

            aab = jnp.where(strict, _bmm_nt(at, bh), 0.0)
            aak = jnp.where(strict, _bmm_nt(at, kh), 0.0).astype(BF16)
            arb = jnp.where(incl, _bmm_nt(rh_bf, bh), 0.0).astype(BF16)
            ark = jnp.where(incl, _bmm_nt(rh_bf, kh), 0.0).astype(BF16)

            tinv = eye + aab
            pw = aab
            step = 2
            while step < L:
                pw_bf = pw.astype(BF16)
                pw = _bmm(pw_bf, pw_bf)
                tinv = tinv + _bmm(tinv.astype(BF16), pw.astype(BF16))
                step *= 2
            tinv = tinv.astype(BF16)

            ap = _bmm(tinv, at).astype(BF16)
            u0 = _bmm(tinv, _bmm(aak, vs).astype(BF16)).astype(BF16)
            rp_s[pl.ds(c0, CG)] = (rh + _bmm(arb, ap)).astype(BF16)
            y1_s[pl.ds(c0, CG)] = _bmm(arb, u0) + _bmm(ark, vs)
            mt_s[pl.ds(c0, CG)] = _bmm_tn(ap, bt).astype(BF16)
            gt_s[pl.ds(c0, CG)] = _bmm_tn(u0, bt) + _bmm_tn(vs, kt)
            dec_s[pl.ds(c0, CG)] = jnp.exp(tot)
            return carry

        lax.fori_loop(0, n_groups, precompute, 0)

        def recur(i, s):
            c = (n_chunks - 1 - i) if reverse else i
            s_bf = s.astype(BF16)
            ss_s[c] = s_bf
            return s * dec_s[c] + _dot(s_bf, mt_s[c]) + gt_s[c]

        lax.fori_loop(0, n_chunks, recur, jnp.zeros((LANES, LANES), F32))

        def readout(gi, carry):
            t0 = pl.multiple_of(gi * GT, GT)
            c0 = pl.multiple_of(gi * CG, CG)
            ys = _bmm_nt(rp_s[pl.ds(c0, CG)], ss_s[pl.ds(c0, CG)]) + y1_s[pl.ds(c0, CG)]
            y = (ys[:, :L, :] + ys[:, L:, :]).reshape(GT, LANES)
            if d == 0:
                y_s[pl.ds(t0, GT), :] = y
            else:
                y_s[pl.ds(t0, GT), :] += y
            return carry

        lax.fori_loop(0, n_groups, readout, 0)

    y = y_s[...]
    mean = head_sum(y) * (1.0 / HEAD_DIM)
    yc = y - mean
    var = head_sum(yc * yc) * (1.0 / HEAD_DIM)
    yn = yc * lax.rsqrt(var + LNX_EPS) * lng_ref[...] + lnb_ref[...]
    gate = gate_ref[0].astype(F32)
    o_ref[0] = ((yn + bonus) * _silu(gate)).astype(o_ref.dtype)


def _rwkv(proj3, p):
    B, T, _ = proj3.shape
    n_pairs = B_WIDTH // LANES
    base = (A_WIDTH + 2 * A_KV_WIDTH + A_WIDTH) // LANES
    n_chunks = T // RWKV_CHUNK
    L2 = 2 * RWKV_CHUNK
    vec = lambda a: a.reshape(1, -1)
    blk = lambda off: pl.BlockSpec((1, T, LANES), lambda b, h: (b, 0, off + h))
    pvec = lambda off=0: pl.BlockSpec((1, LANES), lambda b, h: (0, off + h))
    mu = vec(p["mu"])
    in_specs = [
        blk(base), blk(base + n_pairs), blk(base + 2 * n_pairs),
        pl.BlockSpec((1, T, 2 * LANES), lambda b, h: (b, 0, (base + 3 * n_pairs) // 2)),
        blk(base + 3 * n_pairs + 2),
        pvec(0), pvec(n_pairs), pvec(2 * n_pairs),
        pl.BlockSpec((1, 2 * LANES), lambda b, h: (0, 3 * n_pairs // 2)),
        pl.BlockSpec((1, 2 * LANES, 4 * LANES), lambda b, h: (h, 0, 0)),
    ] + [pvec()] * 9
    scratch = [
        pltpu.VMEM((T, LANES), F32),
        pltpu.VMEM((T, LANES), F32),
        pltpu.VMEM((T, LANES), F32),
        pltpu.VMEM((2, T, LANES), F32),
        pltpu.VMEM((2, T, LANES), F32),
        pltpu.VMEM((2, T, LANES), F32),
        pltpu.VMEM((T, LANES), F32),
        pltpu.VMEM((n_chunks, LANES, LANES), BF16),
        pltpu.VMEM((n_chunks, LANES, LANES), F32),
        pltpu.VMEM((n_chunks, L2, LANES), BF16),
        pltpu.VMEM((n_chunks, L2, LANES), F32),
        pltpu.VMEM((n_chunks, 1, LANES), F32),
        pltpu.VMEM((n_chunks, LANES, LANES), BF16),
    ]
    return pl.pallas_call(
        _rwkv_kernel,
        grid=(B, n_pairs),
        in_specs=in_specs,
        out_specs=pl.BlockSpec((1, T, LANES), lambda b, h: (b, 0, h)),
        out_shape=jax.ShapeDtypeStruct((B, T, B_WIDTH), BF16),
        scratch_shapes=scratch,
        compiler_params=_cparams(("parallel", "parallel")),
        name="rwkv7",
    )(proj3, proj3, proj3, proj3, proj3, mu, mu, mu, mu, p["wbig"],
      vec(p["w0_f"]), vec(p["a0_f"]), vec(p["w0_b"]), vec(p["a0_b"]),
      vec(p["k_k"]), vec(p["k_a"]), vec(p["r_k"]), vec(p["lnx_g"]), vec(p["lnx_b"]))


def _lora_weights(w2_f, w2_b, a2_f, a2_b):
    n_pairs = B_WIDTH // LANES
    out = jnp.zeros((n_pairs, 4 * LORA, 4 * LANES), F32)
    for i, w in enumerate((w2_f, w2_b, a2_f, a2_b)):
        wp = w.reshape(LORA, n_pairs, LANES).transpose(1, 0, 2)
        out = out.at[:, i * LORA:(i + 1) * LORA, i * LANES:(i + 1) * LANES].set(wp)
    return out.astype(BF16)


def _ret_kernel(lg_ref, q_ref, k_ref, v_ref, g_ref, gn_ref, o_ref, st_s, cb_s):
    h = pl.program_id(1)
    T = q_ref.shape[1]
    C = min(RET_CHUNK, T)
    n = T // C
    lgf = lg_ref[h, 0]
    lgb = lg_ref[h, 1]
    idx = lax.broadcasted_iota(jnp.int32, (C, 1), 0).astype(F32)
    ji = lax.broadcasted_iota(jnp.int32, (C, C), 0)
    li = lax.broadcasted_iota(jnp.int32, (C, C), 1)
    diff = (ji - li).astype(F32)
    dmat = jnp.exp(jnp.where(ji >= li, diff * lgf, -diff * lgb))
    qdec_f = jnp.exp((idx + 1.0) * lgf)
    kdec_f = jnp.exp((C - 1.0 - idx) * lgf)
    qdec_b = jnp.exp((C - 1.0 - idx) * lgb)
    kdec_b = jnp.exp((idx + 1.0) * lgb)
    cdec_f = jnp.exp(jnp.full((1, 1), float(C), F32) * lgf)
    cdec_b = jnp.exp(jnp.full((1, 1), float(C), F32) * lgb)

    def chunk(i):
        sl = pl.ds(pl.multiple_of(i * C, C), C)
        return sl, q_ref[0, sl, :], k_ref[0, sl, :], v_ref[0, sl, :]

    st_s[...] = jnp.zeros_like(st_s)

    def bwd(i, carry):
        sl, q, k, v = chunk(n - 1 - i)
        st = st_s[...]
        cb_s[sl, :] = _dot((q * qdec_b).astype(BF16), st.astype(BF16))
        st_s[...] = st * cdec_b + _dot_tn((k * kdec_b).astype(BF16), v)
        return carry

    lax.fori_loop(0, n, bwd, 0)

    st_s[...] = jnp.zeros_like(st_s)
    inv_n = 1.0 / o_ref.shape[-1]

    def fwd(i, carry):
        sl, q, k, v = chunk(i)
        st = st_s[...]
        s = _dot_nt(q, k) * dmat
        o = _dot(s.astype(BF16), v) + _dot((q * qdec_f).astype(BF16), st.astype(BF16)) + cb_s[sl, :]
        st_s[...] = st * cdec_f + _dot_tn((k * kdec_f).astype(BF16), v)
        mean = jnp.sum(o, axis=-1, keepdims=True) * inv_n
        oc = o - mean
        var = jnp.sum(oc * oc, axis=-1, keepdims=True) * inv_n
        y = oc * lax.rsqrt(var + GN_EPS) * gn_ref[...]
        o_ref[0, sl, :] = (_silu(g_ref[0, sl, :].astype(F32)) * y).astype(o_ref.dtype)
        return carry

    lax.fori_loop(0, n, fwd, 0)


def _retention(proj3, gn_g):
    B, T, width = proj3.shape
    dk = (width // 6) // C_HEADS
    dv = 2 * dk
    lg_f = np.log(1.0 - 2.0 ** (-5.0 - np.arange(C_HEADS, dtype=np.float32))).astype(np.float32)
    lg = jnp.asarray(np.stack([lg_f, lg_f[::-1]], axis=1), F32)
    return pl.pallas_call(
        _ret_kernel,
        grid=(B, C_HEADS),
        in_specs=[
            pl.BlockSpec(memory_space=pltpu.SMEM),
            pl.BlockSpec((1, T, dk), lambda b, h: (b, 0, h)),
            pl.BlockSpec((1, T, dk), lambda b, h: (b, 0, C_HEADS + h)),
            pl.BlockSpec((1, T, dv), lambda b, h: (b, 0, C_HEADS + h)),
            pl.BlockSpec((1, T, dv), lambda b, h: (b, 0, 2 * C_HEADS + h)),
            pl.BlockSpec((1, dv), lambda b, h: (0, h)),
        ],
        out_specs=pl.BlockSpec((1, T, dv), lambda b, h: (b, 0, h)),
        out_shape=jax.ShapeDtypeStruct((B, T, C_HEADS * dv), BF16),
        scratch_shapes=[pltpu.VMEM((dk, dv), F32), pltpu.VMEM((T, dv), F32)],
        compiler_params=_cparams(("parallel", "parallel")),
        name="retention",
    )(lg, proj3, proj3, proj3, proj3, gn_g.reshape(1, -1))


def kernel(x, pre_gain, post_gain, even_w_in, even_mu, even_q_gain, even_k_gain, even_k_k, even_k_a, even_r_k, even_w0_f, even_w2_f, even_a0_f, even_a2_f, even_w0_b, even_w2_b, even_a0_b, even_a2_b, even_lnx_g, even_lnx_b, even_w_out, odd_w_in, odd_gn_g, odd_w_out):
    B, T, D = x.shape
    n_tok = B * T
    depth = pre_gain.shape[0]
    tm = min(512, T)
    tq = min(256, T)
    cos_a, sin_a = _rope_tables(T, HEAD_DIM)
    cos_a, sin_a = jnp.tile(cos_a, (1, 2)), jnp.tile(sin_a, (1, 2))
    dk = odd_w_in.shape[-1] // 6 // C_HEADS
    cos_c, sin_c = _rope_tables(T, dk)

    h = x.reshape(n_tok, D)
    for layer in range(depth):
        i = layer // 2
        if layer % 2 == 0:
            proj = _proj(h, pre_gain[layer], even_w_in[i].astype(BF16), tm=tm, col_chunk=512, seq=T)
            proj3 = proj.reshape(B, T, -1)
            out_a = _attention(proj3, cos_a, sin_a, even_q_gain[i], even_k_gain[i], tq=tq)
            params = dict(
                mu=even_mu[i], k_k=even_k_k[i], k_a=even_k_a[i], r_k=even_r_k[i],
                w0_f=even_w0_f[i], a0_f=even_a0_f[i], w0_b=even_w0_b[i], a0_b=even_a0_b[i],
                lnx_g=even_lnx_g[i], lnx_b=even_lnx_b[i],
                wbig=_lora_weights(even_w2_f[i], even_w2_b[i], even_a2_f[i], even_a2_b[i]),
            )
            out_b = _rwkv(proj3, params)
            mixed = [out_a.reshape(n_tok, -1), out_b.reshape(n_tok, -1)]
            h = _out_proj(mixed, even_w_out[i].astype(BF16), post_gain[layer], h, tm=tm)
        else:
            proj = _proj(h, pre_gain[layer], odd_w_in[i].astype(BF16), tm=tm, col_chunk=1024, seq=T,
                         rope=(cos_c, sin_c, 2 * C_HEADS * dk), k_scale=dk ** -0.5)
            ret = _retention(proj.reshape(B, T, -1), odd_gn_g[i])
            h = _out_proj([ret.reshape(n_tok, -1)], odd_w_out[i].astype(BF16), post_gain[layer], h, tm=tm)
    return h.reshape(B, T, D)
```

```python
import functools

import numpy as np
import jax
import jax.numpy as jnp
from jax import lax
from jax.experimental import pallas as pl
from jax.experimental.pallas import tpu as pltpu

F32 = jnp.float32
BF16 = jnp.bfloat16

GRID_W = 64
ROPE_THETA = 10000.0
NORM_EPS = 1e-6

A_HEADS = 8
A_KV_HEADS = 2
HEAD_DIM = 64
A_WIDTH = A_HEADS * HEAD_DIM
A_KV_WIDTH = A_KV_HEADS * HEAD_DIM
B_HEADS = 8
B_WIDTH = B_HEADS * HEAD_DIM
LORA = 64
LNX_EPS = 64e-5
C_HEADS = 4
GN_EPS = 1e-5

LANES = 128
RWKV_CHUNK = 64
RWKV_GROUP = 8
RET_CHUNK = 256
VMEM_LIMIT = 56 * 1024 * 1024


def _cparams(sem):
    return pltpu.CompilerParams(dimension_semantics=sem, vmem_limit_bytes=VMEM_LIMIT)


def _dot(a, b):
    return jnp.dot(a, b, preferred_element_type=F32)


def _dot_nt(a, b):
    return lax.dot_general(a, b, (((1,), (1,)), ((), ())), preferred_element_type=F32)


def _dot_tn(a, b):
    return lax.dot_general(a, b, (((0,), (0,)), ((), ())), preferred_element_type=F32)


def _bmm(a, b):
    return lax.dot_general(a, b, (((2,), (1,)), ((0,), (0,))), preferred_element_type=F32)


def _bmm_nt(a, b):
    return lax.dot_general(a, b, (((2,), (2,)), ((0,), (0,))), preferred_element_type=F32)


def _bmm_tn(a, b):
    return lax.dot_general(a, b, (((1,), (1,)), ((0,), (0,))), preferred_element_type=F32)


def _sigmoid(x):
    return 1.0 / (1.0 + jnp.exp(-x))


def _silu(x):
    return x * _sigmoid(x)


def _rope_tables(T, dim):
    half = dim // 2
    t = np.arange(T)
    row = (t // GRID_W).astype(np.float32)
    col = (t % GRID_W).astype(np.float32)
    inv_freq = (ROPE_THETA ** (-np.arange(0, half, 2, dtype=np.float32) / half)).astype(np.float32)
    ar = row[:, None] * inv_freq
    ac = col[:, None] * inv_freq
    cos = np.concatenate([np.cos(ar), np.cos(ar), np.cos(ac), np.cos(ac)], axis=1)
    sin = np.concatenate([-np.sin(ar), np.sin(ar), -np.sin(ac), np.sin(ac)], axis=1)
    return jnp.asarray(cos, F32), jnp.asarray(sin, F32)


def _proj_kernel(x_ref, g_ref, w_ref, *rest, col_chunk, rope_cols, k_scale):
    if rope_cols:
        cos_ref, sin_ref, o_ref = rest
    else:
        (o_ref,) = rest
    x = x_ref[...]
    ms = jnp.mean(x * x, axis=-1, keepdims=True)
    hn = (x * lax.rsqrt(ms + NORM_EPS) * g_ref[...]).astype(BF16)
    n_out = o_ref.shape[-1]
    for c0 in range(0, n_out, col_chunk):
        acc = _dot(hn, w_ref[:, c0:c0 + col_chunk])
        if c0 < rope_cols:
            scale = 1.0 if c0 < rope_cols // 2 else k_scale
            for b0 in range(0, col_chunk, LANES):
                xb = acc[:, b0:b0 + LANES]
                t0 = ((c0 + b0) % (2 * LANES))
                cb = cos_ref[:, t0:t0 + LANES]
                sb = sin_ref[:, t0:t0 + LANES]
                yb = xb * cb + pltpu.roll(xb, LANES // 2, 1) * sb
                o_ref[:, c0 + b0:c0 + b0 + LANES] = (yb * scale).astype(o_ref.dtype)
        else:
            o_ref[:, c0:c0 + col_chunk] = acc.astype(o_ref.dtype)


def _proj(h2, gain, w_bf, *, tm, col_chunk, seq, rope=None, k_scale=1.0):
    n_tok, d = h2.shape
    n_out = w_bf.shape[1]
    in_specs = [
        pl.BlockSpec((tm, d), lambda i: (i, 0)),
        pl.BlockSpec((1, d), lambda i: (0, 0)),
        pl.BlockSpec((d, n_out), lambda i: (0, 0)),
    ]
    args = [h2, gain.reshape(1, d), w_bf]
    rope_cols = 0
    if rope is not None:
        cos, sin, rope_cols = rope
        tiles_per_seq = seq // tm
        in_specs += [pl.BlockSpec((tm, cos.shape[1]), lambda i: (i % tiles_per_seq, 0))] * 2
        args += [cos, sin]
    return pl.pallas_call(
        functools.partial(_proj_kernel, col_chunk=col_chunk, rope_cols=rope_cols, k_scale=k_scale),
        grid=(n_tok // tm,),
        in_specs=in_specs,
        out_specs=pl.BlockSpec((tm, n_out), lambda i: (i, 0)),
        out_shape=jax.ShapeDtypeStruct((n_tok, n_out), BF16),
        compiler_params=_cparams(("parallel",)),
        name="proj",
    )(*args)


def _out_kernel(*refs, n_in):
    ins = refs[:n_in]
    w_ref, g_ref, h_ref, o_ref = refs[n_in:]
    acc = None
    r0 = 0
    for m_ref in ins:
        k = m_ref.shape[-1]
        part = _dot(m_ref[...], w_ref[r0:r0 + k, :])
        acc = part if acc is None else acc + part
        r0 += k
    ms = jnp.mean(acc * acc, axis=-1, keepdims=True)
    o_ref[...] = h_ref[...] + acc * lax.rsqrt(ms + NORM_EPS) * g_ref[...]


def _out_proj(mixed, w_bf, gain, h2, *, tm):
    n_tok, d = h2.shape
    k_total = w_bf.shape[0]
    in_specs = [pl.BlockSpec((tm, m.shape[1]), lambda i: (i, 0)) for m in mixed]
    in_specs += [
        pl.BlockSpec((k_total, d), lambda i: (0, 0)),
        pl.BlockSpec((1, d), lambda i: (0, 0)),
        pl.BlockSpec((tm, d), lambda i: (i, 0)),
    ]
    return pl.pallas_call(
        functools.partial(_out_kernel, n_in=len(mixed)),
        grid=(n_tok // tm,),
        in_specs=in_specs,
        out_specs=pl.BlockSpec((tm, d), lambda i: (i, 0)),
        out_shape=jax.ShapeDtypeStruct((n_tok, d), F32),
        compiler_params=_cparams(("parallel",)),
        name="out_proj",
    )(*mixed, w_bf, gain.reshape(1, d), h2)


def _attn_kernel(q_ref, kv_ref, ga_ref, gb_ref, cos_ref, sin_ref, qg_ref, kg_ref, o_ref, kn_ref, vx_ref, *, tq):
    t_idx = pl.program_id(1)
    lane = lax.broadcasted_iota(jnp.int32, (1, LANES), 1)
    head0 = lane < HEAD_DIM
    first = (lane % (HEAD_DIM // 2)) < (HEAD_DIM // 4)

    def norm_rope(x, gain, cos, sin):
        sq = x * x
        s0 = jnp.sum(jnp.where(head0, sq, 0.0), axis=-1, keepdims=True)
        s1 = jnp.sum(jnp.where(head0, 0.0, sq), axis=-1, keepdims=True)
        ms = jnp.where(head0, s0, s1) * (1.0 / HEAD_DIM)
        y = x * lax.rsqrt(ms + NORM_EPS) * gain
        q4 = HEAD_DIM // 4
        partner = jnp.where(first, pltpu.roll(y, LANES - q4, 1), pltpu.roll(y, q4, 1))
        return y * cos + partner * sin

    @pl.when(t_idx == 0)
    def _():
        k = kv_ref[0, :, 0:LANES].astype(F32)
        kn_ref[...] = norm_rope(k, kg_ref[...], cos_ref[...], sin_ref[...]).astype(BF16)
        vx_ref[:, 0:LANES] = kv_ref[0, :, LANES:2 * LANES]
        vx_ref[:, LANES:2 * LANES] = jnp.ones((kv_ref.shape[1], LANES), BF16)

    row0 = pl.multiple_of(t_idx * tq, tq)
    cos_q = cos_ref[pl.ds(row0, tq), :]
    sin_q = sin_ref[pl.ds(row0, tq), :]
    kn = kn_ref[...]
    vx = vx_ref[...]
    scale = HEAD_DIM ** -0.5 * np.log2(np.e)
    rep = A_HEADS // A_KV_HEADS
    for pair in range(A_HEADS // 2):
        g = (2 * pair) // rep
        qp = q_ref[0, pl.ds(row0, tq), pair * LANES:(pair + 1) * LANES].astype(F32)
        qn = norm_rope(qp, qg_ref[...], cos_q, sin_q) * scale
        qsw = pltpu.roll(qn, HEAD_DIM, 1)
        grp = head0 if g == 0 else jnp.logical_not(head0)
        outs = []
        for sub in range(2):
            src = qn if sub == g else qsw
            qz = jnp.where(grp, src, 0.0).astype(BF16)
            s = _dot_nt(qz, kn)
            m = jnp.max(s, axis=-1, keepdims=True)
            p = jnp.exp2(s - m)
            ox = _dot(p.astype(BF16), vx)
            o = ox[:, 0:LANES] * (1.0 / ox[:, LANES:2 * LANES])
            outs.append(o if sub == g else pltpu.roll(o, HEAD_DIM, 1))
        o_pair = jnp.where(head0, outs[0], outs[1])
        gref = ga_ref if pair < 2 else gb_ref
        gc = (pair % 2) * LANES
        gate = gref[0, pl.ds(row0, tq), gc:gc + LANES].astype(F32)
        o_ref[0, :, pair * LANES:(pair + 1) * LANES] = (o_pair * _silu(gate)).astype(o_ref.dtype)


def _attention(proj3, cos, sin, q_gain, k_gain, *, tq):
    B, T, _ = proj3.shape
    qg = jnp.tile(q_gain, 2).reshape(1, LANES)
    kg = jnp.tile(k_gain, 2).reshape(1, LANES)
    return pl.pallas_call(
        functools.partial(_attn_kernel, tq=tq),
        grid=(B, T // tq),
        in_specs=[
            pl.BlockSpec((1, T, A_WIDTH), lambda b, t: (b, 0, 0)),
            pl.BlockSpec((1, T, 2 * A_KV_WIDTH), lambda b, t: (b, 0, A_WIDTH // (2 * A_KV_WIDTH))),
            pl.BlockSpec((1, T, 2 * LANES), lambda b, t: (b, 0, 3)),
            pl.BlockSpec((1, T, 2 * LANES), lambda b, t: (b, 0, 4)),
            pl.BlockSpec((T, LANES), lambda b, t: (0, 0)),
            pl.BlockSpec((T, LANES), lambda b, t: (0, 0)),
            pl.BlockSpec((1, LANES), lambda b, t: (0, 0)),
            pl.BlockSpec((1, LANES), lambda b, t: (0, 0)),
        ],
        out_specs=pl.BlockSpec((1, tq, A_WIDTH), lambda b, t: (b, t, 0)),
        out_shape=jax.ShapeDtypeStruct((B, T, A_WIDTH), BF16),
        scratch_shapes=[pltpu.VMEM((T, LANES), BF16), pltpu.VMEM((T, 2 * LANES), BF16)],
        compiler_params=_cparams(("parallel", "arbitrary")),
        name="attention",
    )(proj3, proj3, proj3, proj3, cos, sin, qg, kg)


def _rwkv_groups(T):
    n_chunks = T // RWKV_CHUNK
    cg = min(RWKV_GROUP, n_chunks // 2)
    n_groups = n_chunks // cg
    assert n_groups % 2 == 0 and n_groups * cg == n_chunks
    return n_chunks, cg, n_groups


def _rwkv_kernel(r_ref, k_ref, v_ref, lo_ref, gate_ref,
                 mur_ref, muk_ref, muv_ref, mulo_ref, wbig_ref,
                 w0f_ref, a0f_ref, w0b_ref, a0b_ref,
                 kk_ref, ka_ref, rk_ref, lng_ref, lnb_ref,
                 o_ref,
                 r_s, v_s, a_s, k_s, b_s, lw_s, y_s,
                 rp_s, y1_s, ss_s, mt_a, gt_a, dec_a, mt_b, gt_b, dec_b):
    T = r_ref.shape[1]
    L = RWKV_CHUNK
    n_chunks, CG, n_groups = _rwkv_groups(T)
    GT = CG * L
    row = lax.broadcasted_iota(jnp.int32, (T, 1), 0)
    lane = lax.broadcasted_iota(jnp.int32, (1, LANES), 1)
    head0 = lane < HEAD_DIM

    def shift(ref, mu_ref):
        u = ref[0].astype(F32)
        prev = jnp.where(row == 0, 0.0, pltpu.roll(u, 1, 0))
        nxt = jnp.where(row == T - 1, 0.0, pltpu.roll(u, T - 1, 0))
        return u + mu_ref[...] * (0.5 * (prev + nxt) - u)

    def head_sum(x):
        s0 = jnp.sum(jnp.where(head0, x, 0.0), axis=-1, keepdims=True)
        s1 = jnp.sum(jnp.where(head0, 0.0, x), axis=-1, keepdims=True)
        return jnp.where(head0, s0, s1)

    r = shift(r_ref, mur_ref)
    kb = shift(k_ref, muk_ref)
    vb = shift(v_ref, muv_ref)
    lo = shift(lo_ref, mulo_ref)
    lane2 = lax.broadcasted_iota(jnp.int32, (1, 2 * LANES), 1)
    lo_t = jnp.where(lane2 < LANES, jnp.tanh(lo), lo).astype(BF16)
    z = _dot(lo_t, wbig_ref[0])

    kkv = kb * kk_ref[...]
    kkn = kkv / jnp.maximum(jnp.sqrt(head_sum(kkv * kkv)), 1e-12)
    bonus = head_sum(r * kb * rk_ref[...]) * vb

    r_s[...] = r
    v_s[...] = vb
    a_s[...] = -kkn
    for d, (w0_ref, a0_ref) in enumerate(((w0f_ref, a0f_ref), (w0b_ref, a0b_ref))):
        xw = w0_ref[...] + z[:, d * LANES:(d + 1) * LANES]
        softplus = jnp.maximum(-xw, 0.0) + jnp.log(1.0 + jnp.exp(-jnp.abs(xw)))
        lw_s[d] = -jnp.exp(-softplus - 0.5)
        iclr = _sigmoid(a0_ref[...] + z[:, (2 + d) * LANES:(3 + d) * LANES])
        k_s[d] = kb * (1.0 + (iclr - 1.0) * ka_ref[...])
        b_s[d] = kkn * iclr

    pos = lax.broadcasted_iota(jnp.int32, (GT, 1), 0) % L
    ti = lax.broadcasted_iota(jnp.int32, (2 * L, 2 * L), 0)
    si = lax.broadcasted_iota(jnp.int32, (2 * L, 2 * L), 1)
    same = (ti // L) == (si // L)
    eye = (ti == si).astype(F32)

    def stack(x):
        return jnp.concatenate([jnp.where(head0, x, 0.0), jnp.where(head0, 0.0, x)], axis=1)

    def precompute(d, gi, mt_o, gt_o, dec_o):
        reverse = d == 1
        strict = jnp.logical_and(same, (ti < si) if reverse else (ti > si))
        incl = jnp.logical_and(same, (ti <= si) if reverse else (ti >= si))
        t0 = pl.multiple_of(gi * GT, GT)
        c0 = pl.multiple_of(gi * CG, CG)
        sl = pl.ds(t0, GT)
        lw = lw_s[d, sl, :]
        cf = lw
        sh = 1
        while sh < L:
            cf = cf + jnp.where(pos >= sh, pltpu.roll(cf, sh, 0), 0.0)
            sh *= 2
        to3 = lambda x: x.reshape(CG, L, LANES)
        cf3, lw3 = to3(cf), to3(lw)
        tot = cf3[:, L - 1:L, :]
        if reverse:
            cin = tot - cf3 + lw3
            cex = tot - cf3
        else:
            cin = cf3
            cex = cf3 - lw3
        a3, r3, v3 = to3(a_s[sl, :]), to3(r_s[sl, :]), to3(v_s[sl, :])
        k3, b3 = to3(k_s[d, sl, :]), to3(b_s[d, sl, :])
        ek = jnp.exp(-cin)
        ekt = jnp.exp(tot - cin)
        at = stack(a3 * jnp.exp(cex)).astype(BF16)
        rh = stack(r3 * jnp.exp(cin))
        bh = stack(b3 * ek).astype(BF16)
        kh = stack(k3 * ek).astype(BF16)
        bt = stack(b3 * ekt).astype(BF16)
        kt = stack(k3 * ekt).astype(BF16)
        vs = stack(v3).astype(BF16)
        rh_bf = rh.astype(BF16)

        aab = jnp.where(strict, _bmm_nt(at, bh), 0.0)
        aak = jnp.where(strict, _bmm_nt(at, kh), 0.0).astype(BF16)
        arb = jnp.where(incl, _bmm_nt(rh_bf, bh), 0.0).astype(BF16)
        ark = jnp.where(incl, _bmm_nt(rh_bf, kh), 0.0).astype(BF16)

        tinv = eye + aab
        pw = aab
        step = 2
        while step < L:
            pw_bf = pw.astype(BF16)
            pw = _bmm(pw_bf, pw_bf)
            tinv = tinv + _bmm(tinv.astype(BF16), pw.astype(BF16))
            step *= 2
        tinv = tinv.astype(BF16)

        ap = _bmm(tinv, at).astype(BF16)
        u0 = _bmm(tinv, _bmm(aak, vs).astype(BF16)).astype(BF16)
        rp_s[d, pl.ds(c0, CG)] = (rh + _bmm(arb, ap)).astype(BF16)
        y1_s[d, pl.ds(c0, CG)] = _bmm(arb, u0) + _bmm(ark, vs)
        mt_o[d] = _bmm_tn(ap, bt).astype(BF16)
        gt_o[d] = _bmm_tn(u0, bt) + _bmm_tn(vs, kt)
        dec_o[d] = jnp.exp(tot)

    def recur_groups(gf, gb, sf, sb, mt_i, gt_i, dec_i):
        for i in range(CG):
            jf, jb = i, CG - 1 - i
            sf_bf = sf.astype(BF16)
            sb_bf = sb.astype(BF16)
            ss_s[0, (gf + 1) * CG + jf] = sf_bf
            ss_s[1, (gb + 1) * CG + jb] = sb_bf
            sf = sf * dec_i[0, jf] + _dot(sf_bf, mt_i[0, jf]) + gt_i[0, jf]
            sb = sb * dec_i[1, jb] + _dot(sb_bf, mt_i[1, jb]) + gt_i[1, jb]
        return sf, sb

    buf_a = (mt_a, gt_a, dec_a)
    buf_b = (mt_b, gt_b, dec_b)
    mt_b[...] = jnp.zeros_like(mt_b)
    gt_b[...] = jnp.zeros_like(gt_b)
    dec_b[...] = jnp.zeros_like(dec_b)

    def pipelined(j, carry):
        sf, sb = carry
        g = 2 * j
        precompute(0, g, *buf_a)
        precompute(1, n_groups - 1 - g, *buf_a)
        sf, sb = recur_groups(g - 1, n_groups - g, sf, sb, *buf_b)
        precompute(0, g + 1, *buf_b)
        precompute(1, n_groups - 2 - g, *buf_b)
        return recur_groups(g, n_groups - 1 - g, sf, sb, *buf_a)

    zero_state = jnp.zeros((LANES, LANES), F32)
    sf, sb = lax.fori_loop(0, n_groups // 2, pipelined, (zero_state, zero_state))
    recur_groups(n_groups - 1, 0, sf, sb, *buf_b)

    def readout(gi, carry):
        t0 = pl.multiple_of(gi * GT, GT)
        c0 = pl.multiple_of(gi * CG, CG)
        ys = (_bmm_nt(rp_s[0, pl.ds(c0, CG)], ss_s[0, pl.ds(c0 + CG, CG)]) + y1_s[0, pl.ds(c0, CG)]
              + _bmm_nt(rp_s[1, pl.ds(c0, CG)], ss_s[1, pl.ds(c0 + CG, CG)]) + y1_s[1, pl.ds(c0, CG)])
        y_s[pl.ds(t0, GT), :] = (ys[:, :L, :] + ys[:, L:, :]).reshape(GT, LANES)
        return carry

    lax.fori_loop(0, n_groups, readout, 0)

    y = y_s[...]
    mean = head_sum(y) * (1.0 / HEAD_DIM)
    yc = y - mean
    var = head_sum(yc * yc) * (1.0 / HEAD_DIM)
    yn = yc * lax.rsqrt(var + LNX_EPS) * lng_ref[...] + lnb_ref[...]
    gate = gate_ref[0].astype(F32)
    o_ref[0] = ((yn + bonus) * _silu(gate)).astype(o_ref.dtype)


def _rwkv(proj3, p):
    B, T, _ = proj3.shape
    n_pairs = B_WIDTH // LANES
    base = (A_WIDTH + 2 * A_KV_WIDTH + A_WIDTH) // LANES
    n_chunks, cg, _ = _rwkv_groups(T)
    L2 = 2 * RWKV_CHUNK
    vec = lambda a: a.reshape(1, -1)
    blk = lambda off: pl.BlockSpec((1, T, LANES), lambda b, h: (b, 0, off + h))
    pvec = lambda off=0: pl.BlockSpec((1, LANES), lambda b, h: (0, off + h))
    mu = vec(p["mu"])
    in_specs = [
        blk(base), blk(base + n_pairs), blk(base + 2 * n_pairs),
        pl.BlockSpec((1, T, 2 * LANES), lambda b, h: (b, 0, (base + 3 * n_pairs) // 2)),
        blk(base + 3 * n_pairs + 2),
        pvec(0), pvec(n_pairs), pvec(2 * n_pairs),
        pl.BlockSpec((1, 2 * LANES), lambda b, h: (0, 3 * n_pairs // 2)),
        pl.BlockSpec((1, 2 * LANES, 4 * LANES), lambda b, h: (h, 0, 0)),
    ] + [pvec()] * 9
    scratch = [
        pltpu.VMEM((T, LANES), F32),
        pltpu.VMEM((T, LANES), F32),
        pltpu.VMEM((T, LANES), F32),
        pltpu.VMEM((2, T, LANES), F32),
        pltpu.VMEM((2, T, LANES), F32),
        pltpu.VMEM((2, T, LANES), F32),
        pltpu.VMEM((T, LANES), F32),
        pltpu.VMEM((2, n_chunks, L2, LANES), BF16),
        pltpu.VMEM((2, n_chunks, L2, LANES), F32),
        pltpu.VMEM((2, n_chunks + 2 * cg, LANES, LANES), BF16),
    ] + 2 * [
        pltpu.VMEM((2, cg, LANES, LANES), BF16),
        pltpu.VMEM((2, cg, LANES, LANES), F32),
        pltpu.VMEM((2, cg, 1, LANES), F32),
    ]
    return pl.pallas_call(
        _rwkv_kernel,
        grid=(B, n_pairs),
        in_specs=in_specs,
        out_specs=pl.BlockSpec((1, T, LANES), lambda b, h: (b, 0, h)),
        out_shape=jax.ShapeDtypeStruct((B, T, B_WIDTH), BF16),
        scratch_shapes=scratch,
        compiler_params=_cparams(("parallel", "parallel")),
        name="rwkv7",
    )(proj3, proj3, proj3, proj3, proj3, mu, mu, mu, mu, p["wbig"],
      vec(p["w0_f"]), vec(p["a0_f"]), vec(p["w0_b"]), vec(p["a0_b"]),
      vec(p["k_k"]), vec(p["k_a"]), vec(p["r_k"]), vec(p["lnx_g"]), vec(p["lnx_b"]))


def _lora_weights(w2_f, w2_b, a2_f, a2_b):
    n_pairs = B_WIDTH // LANES
    out = jnp.zeros((n_pairs, 4 * LORA, 4 * LANES), F32)
    for i, w in enumerate((w2_f, w2_b, a2_f, a2_b)):
        wp = w.reshape(LORA, n_pairs, LANES).transpose(1, 0, 2)
        out = out.at[:, i * LORA:(i + 1) * LORA, i * LANES:(i + 1) * LANES].set(wp)
    return out.astype(BF16)


def _ret_kernel(lg_ref, q_ref, k_ref, v_ref, g_ref, gn_ref, o_ref, st_s, cb_s):
    h = pl.program_id(1)
    T = q_ref.shape[1]
    C = min(RET_CHUNK, T)
    n = T // C
    lgf = lg_ref[h, 0]
    lgb = lg_ref[h, 1]
    idx = lax.broadcasted_iota(jnp.int32, (C, 1), 0).astype(F32)
    ji = lax.broadcasted_iota(jnp.int32, (C, C), 0)
    li = lax.broadcasted_iota(jnp.int32, (C, C), 1)
    diff = (ji - li).astype(F32)
    dmat = jnp.exp(jnp.where(ji >= li, diff * lgf, -diff * lgb))
    dk = q_ref.shape[-1]
    wide = lambda col: jnp.broadcast_to(col, (C, dk)).astype(BF16)
    qdec_f = wide(jnp.exp((idx + 1.0) * lgf))
    kdec_f = wide(jnp.exp((C - 1.0 - idx) * lgf))
    qdec_b = wide(jnp.exp((C - 1.0 - idx) * lgb))
    kdec_b = wide(jnp.exp((idx + 1.0) * lgb))
    cdec_f = jnp.exp(jnp.full((1, 1), float(C), F32) * lgf)
    cdec_b = jnp.exp(jnp.full((1, 1), float(C), F32) * lgb)

    def chunk(i):
        sl = pl.ds(pl.multiple_of(i * C, C), C)
        return sl, q_ref[0, sl, :], k_ref[0, sl, :], v_ref[0, sl, :]

    st_s[...] = jnp.zeros_like(st_s)

    def bwd(i, carry):
        sl, q, k, v = chunk(n - 1 - i)
        st = st_s[...]
        cb_s[sl, :] = _dot(q * qdec_b, st.astype(BF16))
        st_s[...] = st * cdec_b + _dot_tn(k * kdec_b, v)
        return carry

    lax.fori_loop(0, n, bwd, 0, unroll=True)

    st_s[...] = jnp.zeros_like(st_s)
    inv_n = 1.0 / o_ref.shape[-1]

    def fwd(i, carry):
        sl, q, k, v = chunk(i)
        st = st_s[...]
        s = _dot_nt(q, k) * dmat
        o = _dot(s.astype(BF16), v) + _dot(q * qdec_f, st.astype(BF16)) + cb_s[sl, :]
        st_s[...] = st * cdec_f + _dot_tn(k * kdec_f, v)
        mean = jnp.sum(o, axis=-1, keepdims=True) * inv_n
        oc = o - mean
        var = jnp.sum(oc * oc, axis=-1, keepdims=True) * inv_n
        y = oc * lax.rsqrt(var + GN_EPS) * gn_ref[...]
        o_ref[0, sl, :] = (_silu(g_ref[0, sl, :].astype(F32)) * y).astype(o_ref.dtype)
        return carry

    lax.fori_loop(0, n, fwd, 0, unroll=True)


def _retention(proj3, gn_g):
    B, T, width = proj3.shape
    dk = (width // 6) // C_HEADS
    dv = 2 * dk
    lg_f = np.log(1.0 - 2.0 ** (-5.0 - np.arange(C_HEADS, dtype=np.float32))).astype(np.float32)
    lg = jnp.asarray(np.stack([lg_f, lg_f[::-1]], axis=1), F32)
    return pl.pallas_call(
        _ret_kernel,
        grid=(B, C_HEADS),
        in_specs=[
            pl.BlockSpec(memory_space=pltpu.SMEM),
            pl.BlockSpec((1, T, dk), lambda b, h: (b, 0, h)),
            pl.BlockSpec((1, T, dk), lambda b, h: (b, 0, C_HEADS + h)),
            pl.BlockSpec((1, T, dv), lambda b, h: (b, 0, C_HEADS + h)),
            pl.BlockSpec((1, T, dv), lambda b, h: (b, 0, 2 * C_HEADS + h)),
            pl.BlockSpec((1, dv), lambda b, h: (0, h)),
        ],
        out_specs=pl.BlockSpec((1, T, dv), lambda b, h: (b, 0, h)),
        out_shape=jax.ShapeDtypeStruct((B, T, C_HEADS * dv), BF16),
        scratch_shapes=[pltpu.VMEM((dk, dv), F32), pltpu.VMEM((T, dv), F32)],
        compiler_params=_cparams(("parallel", "parallel")),
        name="retention",
    )(lg, proj3, proj3, proj3, proj3, gn_g.reshape(1, -1))


def kernel(x, pre_gain, post_gain, even_w_in, even_mu, even_q_gain, even_k_gain, even_k_k, even_k_a, even_r_k, even_w0_f, even_w2_f, even_a0_f, even_a2_f, even_w0_b, even_w2_b, even_a0_b, even_a2_b, even_lnx_g, even_lnx_b, even_w_out, odd_w_in, odd_gn_g, odd_w_out):
    B, T, D = x.shape
    n_tok = B * T
    depth = pre_gain.shape[0]
    tm = min(512, T)
    tq = min(256, T)
    cos_a, sin_a = _rope_tables(T, HEAD_DIM)
    cos_a, sin_a = jnp.tile(cos_a, (1, 2)), jnp.tile(sin_a, (1, 2))
    dk = odd_w_in.shape[-1] // 6 // C_HEADS
    cos_c, sin_c = _rope_tables(T, dk)

    h = x.reshape(n_tok, D)
    for layer in range(depth):
        i = layer // 2
        if layer % 2 == 0:
            proj = _proj(h, pre_gain[layer], even_w_in[i].astype(BF16), tm=tm, col_chunk=512, seq=T)
            proj3 = proj.reshape(B, T, -1)
            out_a = _attention(proj3, cos_a, sin_a, even_q_gain[i], even_k_gain[i], tq=tq)
            params = dict(
                mu=even_mu[i], k_k=even_k_k[i], k_a=even_k_a[i], r_k=even_r_k[i],
                w0_f=even_w0_f[i], a0_f=even_a0_f[i], w0_b=even_w0_b[i], a0_b=even_a0_b[i],
                lnx_g=even_lnx_g[i], lnx_b=even_lnx_b[i],
                wbig=_lora_weights(even_w2_f[i], even_w2_b[i], even_a2_f[i], even_a2_b[i]),
            )
            out_b = _rwkv(proj3, params)
            mixed = [out_a.reshape(n_tok, -1), out_b.reshape(n_tok, -1)]
            h = _out_proj(mixed, even_w_out[i].astype(BF16), post_gain[layer], h, tm=tm)
        else:
            proj = _proj(h, pre_gain[layer], odd_w_in[i].astype(BF16), tm=tm, col_chunk=1024, seq=T,
                         rope=(cos_c, sin_c, 2 * C_HEADS * dk), k_scale=dk ** -0.5)
            ret = _retention(proj.reshape(B, T, -1), odd_gn_g[i])
            h = _out_proj([ret.reshape(n_tok, -1)], odd_w_out[i].astype(BF16), post_gain[layer], h, tm=tm)
    return h.reshape(B, T, D)
```

```python
import functools

import numpy as np
import jax
import jax.numpy as jnp
from jax import lax
from jax.experimental import pallas as pl
from jax.experimental.pallas import tpu as pltpu

F32 = jnp.float32
BF16 = jnp.bfloat16

GRID_W = 64
ROPE_THETA = 10000.0
NORM_EPS = 1e-6

A_HEADS = 8
A_KV_HEADS = 2
HEAD_DIM = 64
A_WIDTH = A_HEADS * HEAD_DIM
A_KV_WIDTH = A_KV_HEADS * HEAD_DIM
B_HEADS = 8
B_WIDTH = B_HEADS * HEAD_DIM
LORA = 64
LNX_EPS = 64e-5
C_HEADS = 4
GN_EPS = 1e-5

LANES = 128
MXU_DIM = 256
RWKV_CHUNK = 64
RWKV_GROUP = 4
RET_CHUNK = 256
VMEM_LIMIT = 56 * 1024 * 1024


def _cparams(sem):
    return pltpu.CompilerParams(dimension_semantics=sem, vmem_limit_bytes=VMEM_LIMIT)


def _dot(a, b):
    return jnp.dot(a, b, preferred_element_type=F32)


def _dot_nt(a, b):
    return lax.dot_general(a, b, (((1,), (1,)), ((), ())), preferred_element_type=F32)


def _dot_tn(a, b):
    return lax.dot_general(a, b, (((0,), (0,)), ((), ())), preferred_element_type=F32)


def _bmm(a, b):
    return lax.dot_general(a, b, (((2,), (1,)), ((0,), (0,))), preferred_element_type=F32)


def _bmm_nt(a, b):
    return lax.dot_general(a, b, (((2,), (2,)), ((0,), (0,))), preferred_element_type=F32)


def _bmm_tn(a, b):
    return lax.dot_general(a, b, (((1,), (1,)), ((0,), (0,))), preferred_element_type=F32)


def _sigmoid(x):
    return 1.0 / (1.0 + jnp.exp(-x))


def _silu(x):
    return x * _sigmoid(x)


def _rope_tables(T, dim):
    half = dim // 2
    t = np.arange(T)
    row = (t // GRID_W).astype(np.float32)
    col = (t % GRID_W).astype(np.float32)
    inv_freq = (ROPE_THETA ** (-np.arange(0, half, 2, dtype=np.float32) / half)).astype(np.float32)
    ar = row[:, None] * inv_freq
    ac = col[:, None] * inv_freq
    cos = np.concatenate([np.cos(ar), np.cos(ar), np.cos(ac), np.cos(ac)], axis=1)
    sin = np.concatenate([-np.sin(ar), np.sin(ar), -np.sin(ac), np.sin(ac)], axis=1)
    return jnp.asarray(cos, F32), jnp.asarray(sin, F32)


def _proj_kernel(x_ref, g_ref, w_ref, *rest, col_chunk, rope_cols, k_scale):
    if rope_cols:
        cos_ref, sin_ref, o_ref = rest
    else:
        (o_ref,) = rest
    x = x_ref[...]
    ms = jnp.mean(x * x, axis=-1, keepdims=True)
    hn = (x * lax.rsqrt(ms + NORM_EPS) * g_ref[...]).astype(BF16)
    n_out = o_ref.shape[-1]
    for c0 in range(0, n_out, col_chunk):
        acc = _dot(hn, w_ref[:, c0:c0 + col_chunk])
        if c0 < rope_cols:
            scale = 1.0 if c0 < rope_cols // 2 else k_scale
            for b0 in range(0, col_chunk, LANES):
                xb = acc[:, b0:b0 + LANES]
                t0 = ((c0 + b0) % (2 * LANES))
                cb = cos_ref[:, t0:t0 + LANES]
                sb = sin_ref[:, t0:t0 + LANES]
                yb = xb * cb + pltpu.roll(xb, LANES // 2, 1) * sb
                o_ref[:, c0 + b0:c0 + b0 + LANES] = (yb * scale).astype(o_ref.dtype)
        else:
            o_ref[:, c0:c0 + col_chunk] = acc.astype(o_ref.dtype)


def _proj(h2, gain, w_bf, *, tm, col_chunk, seq, rope=None, k_scale=1.0):
    n_tok, d = h2.shape
    n_out = w_bf.shape[1]
    in_specs = [
        pl.BlockSpec((tm, d), lambda i: (i, 0)),
        pl.BlockSpec((1, d), lambda i: (0, 0)),
        pl.BlockSpec((d, n_out), lambda i: (0, 0)),
    ]
    args = [h2, gain.reshape(1, d), w_bf]
    rope_cols = 0
    if rope is not None:
        cos, sin, rope_cols = rope
        tiles_per_seq = seq // tm
        in_specs += [pl.BlockSpec((tm, cos.shape[1]), lambda i: (i % tiles_per_seq, 0))] * 2
        args += [cos, sin]
    return pl.pallas_call(
        functools.partial(_proj_kernel, col_chunk=col_chunk, rope_cols=rope_cols, k_scale=k_scale),
        grid=(n_tok // tm,),
        in_specs=in_specs,
        out_specs=pl.BlockSpec((tm, n_out), lambda i: (i, 0)),
        out_shape=jax.ShapeDtypeStruct((n_tok, n_out), BF16),
        compiler_params=_cparams(("parallel",)),
        name="proj",
    )(*args)


def _out_kernel(*refs, n_in):
    ins = refs[:n_in]
    w_ref, g_ref, h_ref, o_ref = refs[n_in:]
    acc = None
    r0 = 0
    for m_ref in ins:
        k = m_ref.shape[-1]
        part = _dot(m_ref[...], w_ref[r0:r0 + k, :])
        acc = part if acc is None else acc + part
        r0 += k
    ms = jnp.mean(acc * acc, axis=-1, keepdims=True)
    o_ref[...] = h_ref[...] + acc * lax.rsqrt(ms + NORM_EPS) * g_ref[...]


def _out_proj(mixed, w_bf, gain, h2, *, tm):
    n_tok, d = h2.shape
    k_total = w_bf.shape[0]
    in_specs = [pl.BlockSpec((tm, m.shape[1]), lambda i: (i, 0)) for m in mixed]
    in_specs += [
        pl.BlockSpec((k_total, d), lambda i: (0, 0)),
        pl.BlockSpec((1, d), lambda i: (0, 0)),
        pl.BlockSpec((tm, d), lambda i: (i, 0)),
    ]
    return pl.pallas_call(
        functools.partial(_out_kernel, n_in=len(mixed)),
        grid=(n_tok // tm,),
        in_specs=in_specs,
        out_specs=pl.BlockSpec((tm, d), lambda i: (i, 0)),
        out_shape=jax.ShapeDtypeStruct((n_tok, d), F32),
        compiler_params=_cparams(("parallel",)),
        name="out_proj",
    )(*mixed, w_bf, gain.reshape(1, d), h2)


def _attn_kernel(q_ref, kv_ref, ga_ref, gb_ref, cos_ref, sin_ref, qg_ref, kg_ref, o_ref, kn_ref, vx_ref, *, tq):
    t_idx = pl.program_id(1)
    lane = lax.broadcasted_iota(jnp.int32, (1, LANES), 1)
    head0 = lane < HEAD_DIM
    first = (lane % (HEAD_DIM // 2)) < (HEAD_DIM // 4)

    def norm_rope(x, gain, cos, sin):
        sq = x * x
        s0 = jnp.sum(jnp.where(head0, sq, 0.0), axis=-1, keepdims=True)
        s1 = jnp.sum(jnp.where(head0, 0.0, sq), axis=-1, keepdims=True)
        ms = jnp.where(head0, s0, s1) * (1.0 / HEAD_DIM)
        y = x * lax.rsqrt(ms + NORM_EPS) * gain
        q4 = HEAD_DIM // 4
        partner = jnp.where(first, pltpu.roll(y, LANES - q4, 1), pltpu.roll(y, q4, 1))
        return y * cos + partner * sin

    @pl.when(t_idx == 0)
    def _():
        k = kv_ref[0, :, 0:LANES].astype(F32)
        kn_ref[...] = norm_rope(k, kg_ref[...], cos_ref[...], sin_ref[...]).astype(BF16)
        vx_ref[:, 0:LANES] = kv_ref[0, :, LANES:2 * LANES]
        vx_ref[:, LANES:2 * LANES] = jnp.ones((kv_ref.shape[1], LANES), BF16)

    row0 = pl.multiple_of(t_idx * tq, tq)
    cos_q = cos_ref[pl.ds(row0, tq), :]
    sin_q = sin_ref[pl.ds(row0, tq), :]
    kn = kn_ref[...]
    vx = vx_ref[...]
    scale = HEAD_DIM ** -0.5 * np.log2(np.e)
    rep = A_HEADS // A_KV_HEADS
    for pair in range(A_HEADS // 2):
        g = (2 * pair) // rep
        qp = q_ref[0, pl.ds(row0, tq), pair * LANES:(pair + 1) * LANES].astype(F32)
        qn = norm_rope(qp, qg_ref[...], cos_q, sin_q) * scale
        qsw = pltpu.roll(qn, HEAD_DIM, 1)
        grp = head0 if g == 0 else jnp.logical_not(head0)
        outs = []
        for sub in range(2):
            src = qn if sub == g else qsw
            qz = jnp.where(grp, src, 0.0).astype(BF16)
            s = _dot_nt(qz, kn)
            m = jnp.max(s, axis=-1, keepdims=True)
            p = jnp.exp2(s - m)
            ox = _dot(p.astype(BF16), vx)
            o = ox[:, 0:LANES] * (1.0 / ox[:, LANES:2 * LANES])
            outs.append(o if sub == g else pltpu.roll(o, HEAD_DIM, 1))
        o_pair = jnp.where(head0, outs[0], outs[1])
        gref = ga_ref if pair < 2 else gb_ref
        gc = (pair % 2) * LANES
        gate = gref[0, pl.ds(row0, tq), gc:gc + LANES].astype(F32)
        o_ref[0, :, pair * LANES:(pair + 1) * LANES] = (o_pair * _silu(gate)).astype(o_ref.dtype)


def _attention(proj3, cos, sin, q_gain, k_gain, *, tq):
    B, T, _ = proj3.shape
    qg = jnp.tile(q_gain, 2).reshape(1, LANES)
    kg = jnp.tile(k_gain, 2).reshape(1, LANES)
    return pl.pallas_call(
        functools.partial(_attn_kernel, tq=tq),
        grid=(B, T // tq),
        in_specs=[
            pl.BlockSpec((1, T, A_WIDTH), lambda b, t: (b, 0, 0)),
            pl.BlockSpec((1, T, 2 * A_KV_WIDTH), lambda b, t: (b, 0, A_WIDTH // (2 * A_KV_WIDTH))),
            pl.BlockSpec((1, T, 2 * LANES), lambda b, t: (b, 0, 3)),
            pl.BlockSpec((1, T, 2 * LANES), lambda b, t: (b, 0, 4)),
            pl.BlockSpec((T, LANES), lambda b, t: (0, 0)),
            pl.BlockSpec((T, LANES), lambda b, t: (0, 0)),
            pl.BlockSpec((1, LANES), lambda b, t: (0, 0)),
            pl.BlockSpec((1, LANES), lambda b, t: (0, 0)),
        ],
        out_specs=pl.BlockSpec((1, tq, A_WIDTH), lambda b, t: (b, t, 0)),
        out_shape=jax.ShapeDtypeStruct((B, T, A_WIDTH), BF16),
        scratch_shapes=[pltpu.VMEM((T, LANES), BF16), pltpu.VMEM((T, 2 * LANES), BF16)],
        compiler_params=_cparams(("parallel", "arbitrary")),
        name="attention",
    )(proj3, proj3, proj3, proj3, cos, sin, qg, kg)


RWKV_LANES = MXU_DIM


def _rwkv_groups(T):
    n_chunks = T // RWKV_CHUNK
    cg = min(RWKV_GROUP, n_chunks // 2)
    n_groups = n_chunks // cg
    assert n_groups % 2 == 0 and n_groups * cg == n_chunks
    return n_chunks, cg, n_groups


def _rwkv_kernel(r_ref, k_ref, v_ref, lo_ref, gate_ref,
                 mur_ref, muk_ref, muv_ref, mulo_ref, wbig_ref,
                 w0f_ref, a0f_ref, w0b_ref, a0b_ref,
                 kk_ref, ka_ref, rk_ref, lng_ref, lnb_ref,
                 o_ref,
                 r_s, v_s, a_s, k_s, b_s, lw_s, y_s, st_s, ss_s,
                 mt_a, gt_a, rp_a, y1_a, dec_a, mt_b, gt_b, rp_b, y1_b, dec_b):
    T = r_ref.shape[1]
    L = RWKV_CHUNK
    HW = RWKV_LANES
    assert L == HEAD_DIM
    n_chunks, CG, n_groups = _rwkv_groups(T)
    GT = CG * L
    row = lax.broadcasted_iota(jnp.int32, (T, 1), 0)
    lane = lax.broadcasted_iota(jnp.int32, (1, LANES), 1)
    head0 = lane < HEAD_DIM

    def shift(ref, mu_ref):
        u = ref[0].astype(F32)
        prev = jnp.where(row == 0, 0.0, pltpu.roll(u, 1, 0))
        nxt = jnp.where(row == T - 1, 0.0, pltpu.roll(u, T - 1, 0))
        return u + mu_ref[...] * (0.5 * (prev + nxt) - u)

    def head_sum(x):
        parts = []
        for c0 in range(0, HW, LANES):
            xc = x[:, c0:c0 + LANES]
            s0 = jnp.sum(jnp.where(head0, xc, 0.0), axis=-1, keepdims=True)
            s1 = jnp.sum(jnp.where(head0, 0.0, xc), axis=-1, keepdims=True)
            parts.append(jnp.where(head0, s0, s1))
        return jnp.concatenate(parts, axis=1)

    r = shift(r_ref, mur_ref)
    kb = shift(k_ref, muk_ref)
    vb = shift(v_ref, muv_ref)
    lo = shift(lo_ref, mulo_ref)
    lo_t = jnp.concatenate([jnp.tanh(lo[:, :2 * LORA]), lo[:, 2 * LORA:]], axis=1).astype(BF16)
    z = _dot(lo_t, wbig_ref[0])

    kkv = kb * kk_ref[...]
    kkn = kkv / jnp.maximum(jnp.sqrt(head_sum(kkv * kkv)), 1e-12)
    bonus = head_sum(r * kb * rk_ref[...]) * vb

    r_s[...] = r.astype(BF16)
    v_s[...] = vb.astype(BF16)
    a_s[...] = (-kkn).astype(BF16)
    for d, (w0_ref, a0_ref) in enumerate(((w0f_ref, a0f_ref), (w0b_ref, a0b_ref))):
        xw = w0_ref[...] + z[:, d * HW:(d + 1) * HW]
        lw_s[d] = (-np.exp(-0.5)) * _sigmoid(xw)
        iclr = _sigmoid(a0_ref[...] + z[:, (2 + d) * HW:(3 + d) * HW])
        k_s[d] = (kb * (1.0 + (iclr - 1.0) * ka_ref[...])).astype(BF16)
        b_s[d] = (kkn * iclr).astype(BF16)

    pos = lax.broadcasted_iota(jnp.int32, (GT, 1), 0) % L
    trow = lax.broadcasted_iota(jnp.int32, (L, HW), 0)
    scol = lax.broadcasted_iota(jnp.int32, (L, HW), 1) % L
    eye = (trow == scol).astype(F32)
    blk_r = lax.broadcasted_iota(jnp.int32, (HW, HW), 0) // L
    blk_c = lax.broadcasted_iota(jnp.int32, (HW, HW), 1) // HEAD_DIM
    bdm = jnp.where(blk_r == blk_c, 1.0, 0.0).astype(BF16)

    def bd(x):
        return jnp.concatenate([x] * (HW // L), axis=1) * bdm

    def operands(d, gi):
        reverse = d == 1
        sl = pl.ds(pl.multiple_of(gi * GT, GT), GT)
        lw = lw_s[d, sl, :]
        cf = lw
        sh = 1
        while sh < L:
            cf = cf + jnp.where(pos >= sh, pltpu.roll(cf, sh, 0), 0.0)
            sh *= 2
        to3 = lambda x: x.reshape(CG, L, HW)
        cf3, lw3 = to3(cf), to3(lw)
        tot = cf3[:, L - 1:L, :]
        if reverse:
            cin = tot - cf3 + lw3
            cex = tot - cf3
        else:
            cin = cf3
            cex = cf3 - lw3
        a3, r3 = to3(a_s[sl, :].astype(F32)), to3(r_s[sl, :].astype(F32))
        k3, b3 = to3(k_s[d, sl, :].astype(F32)), to3(b_s[d, sl, :].astype(F32))
        v_bf = to3(v_s[sl, :])
        ek = jnp.exp(-cin)
        ekt = jnp.exp(tot - cin)
        return dict(
            at=(a3 * jnp.exp(cex)).astype(BF16), rh=r3 * jnp.exp(cin),
            bh=(b3 * ek).astype(BF16), kh=(k3 * ek).astype(BF16),
            bt=(b3 * ekt).astype(BF16), kt=(k3 * ekt).astype(BF16),
            v=v_bf, dec=jnp.exp(tot))

    def tri(x, fwd_mask, bwd_mask):
        return jnp.concatenate([jnp.where(fwd_mask, x[:CG], 0.0), jnp.where(bwd_mask, x[CG:], 0.0)], axis=0)

    def precompute(gf, gb, mt_o, gt_o, rp_o, y1_o, dec_o):
        of, ob = operands(0, gf), operands(1, gb)
        cat = lambda name: jnp.concatenate([of[name], ob[name]], axis=0)
        at, rh, bt, kt, v_bf = cat("at"), cat("rh"), cat("bt"), cat("kt"), cat("v")
        vs = bd(v_bf)
        strict = lambda x: tri(x, trow > scol, trow < scol)
        incl = lambda x: tri(x, trow >= scol, trow <= scol)

        lhs = jnp.concatenate([at, rh.astype(BF16)], axis=1)
        sb = _bmm_nt(lhs, bd(cat("bh")))
        sk = _bmm_nt(lhs, bd(cat("kh")))
        aab = strict(sb[:, :L])
        arb = incl(sb[:, L:]).astype(BF16)
        aak = strict(sk[:, :L]).astype(BF16)
        ark = incl(sk[:, L:]).astype(BF16)

        tinv = eye + aab
        aab_bf = aab.astype(BF16)
        pw = _bmm(aab_bf, bd(aab_bf))
        step = 4
        while step < L:
            both = _bmm(jnp.concatenate([tinv.astype(BF16), pw.astype(BF16)], axis=1), bd(pw.astype(BF16)))
            tinv = tinv + both[:, :L]
            pw = both[:, L:]
            step *= 2
        tinv = (tinv + _bmm(tinv.astype(BF16), bd(pw.astype(BF16)))).astype(BF16)

        akv = _bmm(jnp.concatenate([aak, ark], axis=1), vs)
        ap = _bmm(tinv, bd(at)).astype(BF16)
        u0 = _bmm(tinv, bd(akv[:, :L].astype(BF16))).astype(BF16)
        split = lambda x: x.reshape((2, CG) + x.shape[1:])
        rp_o[...] = split((rh + _bmm(arb, bd(ap))).astype(BF16))
        y1_o[...] = split(_bmm(arb, bd(u0)) + akv[:, L:])
        mt_o[...] = split(_bmm_tn(ap, bt).astype(BF16))
        gt_o[...] = split(_bmm_tn(jnp.concatenate([u0, v_bf], axis=1),
                                  jnp.concatenate([bt, kt], axis=1)))
        dec_o[...] = split(cat("dec"))

    def recur_readout(gf, gb, mt_i, gt_i, rp_i, y1_i, dec_i):
        for i in range(CG):
            for d, j in ((0, i), (1, CG - 1 - i)):
                s = st_s[d]
                s_bf = s.astype(BF16) * bdm
                ss_s[d, j] = s_bf
                st_s[d] = s * dec_i[d, j] + _dot(s_bf, mt_i[d, j]) + gt_i[d, j]
        for d, g in ((0, gf), (1, gb)):
            y = _bmm_nt(rp_i[d], ss_s[d]) + y1_i[d]
            rows = pl.ds(pl.multiple_of((g + 1) * GT, GT), GT)
            y_s[rows, :] += y.reshape(GT, HW)

    buf_a = (mt_a, gt_a, rp_a, y1_a, dec_a)
    buf_b = (mt_b, gt_b, rp_b, y1_b, dec_b)
    for ref in buf_b + (st_s, y_s):
        ref[...] = jnp.zeros_like(ref)

    def pipelined(j, carry):
        g = 2 * j
        precompute(g, n_groups - 1 - g, *buf_a)
        recur_readout(g - 1, n_groups - g, *buf_b)
        precompute(g + 1, n_groups - 2 - g, *buf_b)
        recur_readout(g, n_groups - 1 - g, *buf_a)
        return carry

    lax.fori_loop(0, n_groups // 2, pipelined, 0)
    recur_readout(n_groups - 1, 0, *buf_b)

    y = y_s[GT:GT + T, :]
    mean = head_sum(y) * (1.0 / HEAD_DIM)
    yc = y - mean
    var = head_sum(yc * yc) * (1.0 / HEAD_DIM)
    yn = yc * lax.rsqrt(var + LNX_EPS) * lng_ref[...] + lnb_ref[...]
    gate = gate_ref[0].astype(F32)
    o_ref[0] = ((yn + bonus) * _silu(gate)).astype(o_ref.dtype)


def _rwkv(proj3, p):
    B, T, _ = proj3.shape
    HW = RWKV_LANES
    L = RWKV_CHUNK
    n_q = B_WIDTH // HW
    base = (A_WIDTH + 2 * A_KV_WIDTH + A_WIDTH) // HW
    _, cg, n_groups = _rwkv_groups(T)
    vec = lambda a: a.reshape(1, -1)
    blk = lambda off: pl.BlockSpec((1, T, HW), lambda b, h: (b, 0, off + h))
    pvec = lambda off=0: pl.BlockSpec((1, HW), lambda b, h: (0, off + h))
    mu = vec(p["mu"])
    in_specs = [
        blk(base), blk(base + n_q), blk(base + 2 * n_q),
        pl.BlockSpec((1, T, 4 * LORA), lambda b, h: (b, 0, base + 3 * n_q)),
        blk(base + 3 * n_q + 1),
        pvec(0), pvec(n_q), pvec(2 * n_q),
        pl.BlockSpec((1, 4 * LORA), lambda b, h: (0, 3 * n_q)),
        pl.BlockSpec((1, 4 * LORA, 4 * HW), lambda b, h: (h, 0, 0)),
    ] + [pvec()] * 9
    group_bufs = [
        pltpu.VMEM((2, cg, HW, HW), BF16),
        pltpu.VMEM((2, cg, HW, HW), F32),
        pltpu.VMEM((2, cg, L, HW), BF16),
        pltpu.VMEM((2, cg, L, HW), F32),
        pltpu.VMEM((2, cg, 1, HW), F32),
    ]
    scratch = [
        pltpu.VMEM((T, HW), BF16),
        pltpu.VMEM((T, HW), BF16),
        pltpu.VMEM((T, HW), BF16),
        pltpu.VMEM((2, T, HW), BF16),
        pltpu.VMEM((2, T, HW), BF16),
        pltpu.VMEM((2, T, HW), F32),
        pltpu.VMEM(((n_groups + 2) * cg * L, HW), F32),
        pltpu.VMEM((2, HW, HW), F32),
        pltpu.VMEM((2, cg, HW, HW), BF16),
    ] + group_bufs + group_bufs
    return pl.pallas_call(
        _rwkv_kernel,
        grid=(B, n_q),
        in_specs=in_specs,
        out_specs=pl.BlockSpec((1, T, HW), lambda b, h: (b, 0, h)),
        out_shape=jax.ShapeDtypeStruct((B, T, B_WIDTH), BF16),
        scratch_shapes=scratch,
        compiler_params=_cparams(("parallel", "parallel")),
        name="rwkv7",
    )(proj3, proj3, proj3, proj3, proj3, mu, mu, mu, mu, p["wbig"],
      vec(p["w0_f"]), vec(p["a0_f"]), vec(p["w0_b"]), vec(p["a0_b"]),
      vec(p["k_k"]), vec(p["k_a"]), vec(p["r_k"]), vec(p["lnx_g"]), vec(p["lnx_b"]))


def _lora_weights(w2_f, w2_b, a2_f, a2_b):
    HW = RWKV_LANES
    n_q = B_WIDTH // HW
    out = jnp.zeros((n_q, 4 * LORA, 4 * HW), F32)
    for i, w in enumerate((w2_f, w2_b, a2_f, a2_b)):
        wp = w.reshape(LORA, n_q, HW).transpose(1, 0, 2)
        out = out.at[:, i * LORA:(i + 1) * LORA, i * HW:(i + 1) * HW].set(wp)
    return out.astype(BF16)


def _ret_kernel(lg_ref, q_ref, k_ref, v_ref, g_ref, gn_ref, o_ref, st_s, cb_s):
    h = pl.program_id(1)
    T = q_ref.shape[1]
    C = min(RET_CHUNK, T)
    n = T // C
    lgf = lg_ref[h, 0]
    lgb = lg_ref[h, 1]
    idx = lax.broadcasted_iota(jnp.int32, (C, 1), 0).astype(F32)
    ji = lax.broadcasted_iota(jnp.int32, (C, C), 0)
    li = lax.broadcasted_iota(jnp.int32, (C, C), 1)
    diff = (ji - li).astype(F32)
    dmat = jnp.exp(jnp.where(ji >= li, diff * lgf, -diff * lgb))
    dk = q_ref.shape[-1]
    wide = lambda col: jnp.broadcast_to(col, (C, dk)).astype(BF16)
    qdec_f = wide(jnp.exp((idx + 1.0) * lgf))
    kdec_f = wide(jnp.exp((C - 1.0 - idx) * lgf))
    qdec_b = wide(jnp.exp((C - 1.0 - idx) * lgb))
    kdec_b = wide(jnp.exp((idx + 1.0) * lgb))
    cdec_f = jnp.exp(jnp.full((1, 1), float(C), F32) * lgf)
    cdec_b = jnp.exp(jnp.full((1, 1), float(C), F32) * lgb)

    def chunk(i):
        sl = pl.ds(pl.multiple_of(i * C, C), C)
        return sl, q_ref[0, sl, :], k_ref[0, sl, :], v_ref[0, sl, :]

    st_s[...] = jnp.zeros_like(st_s)

    def bwd(i, carry):
        sl, q, k, v = chunk(n - 1 - i)
        st = st_s[...]
        cb_s[sl, :] = _dot(q * qdec_b, st.astype(BF16))
        st_s[...] = st * cdec_b + _dot_tn(k * kdec_b, v)
        return carry

    lax.fori_loop(0, n, bwd, 0, unroll=True)

    st_s[...] = jnp.zeros_like(st_s)
    inv_n = 1.0 / o_ref.shape[-1]

    def fwd(i, carry):
        sl, q, k, v = chunk(i)
        st = st_s[...]
        s = _dot_nt(q, k) * dmat
        o = _dot(s.astype(BF16), v) + _dot(q * qdec_f, st.astype(BF16)) + cb_s[sl, :]
        st_s[...] = st * cdec_f + _dot_tn(k * kdec_f, v)
        mean = jnp.sum(o, axis=-1, keepdims=True) * inv_n
        oc = o - mean
        var = jnp.sum(oc * oc, axis=-1, keepdims=True) * inv_n
        y = oc * lax.rsqrt(var + GN_EPS) * gn_ref[...]
        o_ref[0, sl, :] = (_silu(g_ref[0, sl, :].astype(F32)) * y).astype(o_ref.dtype)
        return carry

    lax.fori_loop(0, n, fwd, 0, unroll=True)


def _retention(proj3, gn_g):
    B, T, width = proj3.shape
    dk = (width // 6) // C_HEADS
    dv = 2 * dk
    lg_f = np.log(1.0 - 2.0 ** (-5.0 - np.arange(C_HEADS, dtype=np.float32))).astype(np.float32)
    lg = jnp.asarray(np.stack([lg_f, lg_f[::-1]], axis=1), F32)
    return pl.pallas_call(
        _ret_kernel,
        grid=(B, C_HEADS),
        in_specs=[
            pl.BlockSpec(memory_space=pltpu.SMEM),
            pl.BlockSpec((1, T, dk), lambda b, h: (b, 0, h)),
            pl.BlockSpec((1, T, dk), lambda b, h: (b, 0, C_HEADS + h)),
            pl.BlockSpec((1, T, dv), lambda b, h: (b, 0, C_HEADS + h)),
            pl.BlockSpec((1, T, dv), lambda b, h: (b, 0, 2 * C_HEADS + h)),
            pl.BlockSpec((1, dv), lambda b, h: (0, h)),
        ],
        out_specs=pl.BlockSpec((1, T, dv), lambda b, h: (b, 0, h)),
        out_shape=jax.ShapeDtypeStruct((B, T, C_HEADS * dv), BF16),
        scratch_shapes=[pltpu.VMEM((dk, dv), F32), pltpu.VMEM((T, dv), F32)],
        compiler_params=_cparams(("parallel", "parallel")),
        name="retention",
    )(lg, proj3, proj3, proj3, proj3, gn_g.reshape(1, -1))


def kernel(x, pre_gain, post_gain, even_w_in, even_mu, even_q_gain, even_k_gain, even_k_k, even_k_a, even_r_k, even_w0_f, even_w2_f, even_a0_f, even_a2_f, even_w0_b, even_w2_b, even_a0_b, even_a2_b, even_lnx_g, even_lnx_b, even_w_out, odd_w_in, odd_gn_g, odd_w_out):
    B, T, D = x.shape
    n_tok = B * T
    depth = pre_gain.shape[0]
    tm = min(512, T)
    tq = min(256, T)
    cos_a, sin_a = _rope_tables(T, HEAD_DIM)
    cos_a, sin_a = jnp.tile(cos_a, (1, 2)), jnp.tile(sin_a, (1, 2))
    dk = odd_w_in.shape[-1] // 6 // C_HEADS
    cos_c, sin_c = _rope_tables(T, dk)

    h = x.reshape(n_tok, D)
    for layer in range(depth):
        i = layer // 2
        if layer % 2 == 0:
            proj = _proj(h, pre_gain[layer], even_w_in[i].astype(BF16), tm=tm, col_chunk=512, seq=T)
            proj3 = proj.reshape(B, T, -1)
            out_a = _attention(proj3, cos_a, sin_a, even_q_gain[i], even_k_gain[i], tq=tq)
            params = dict(
                mu=even_mu[i], k_k=even_k_k[i], k_a=even_k_a[i], r_k=even_r_k[i],
                w0_f=even_w0_f[i], a0_f=even_a0_f[i], w0_b=even_w0_b[i], a0_b=even_a0_b[i],
                lnx_g=even_lnx_g[i], lnx_b=even_lnx_b[i],
                wbig=_lora_weights(even_w2_f[i], even_w2_b[i], even_a2_f[i], even_a2_b[i]),
            )
            out_b = _rwkv(proj3, params)
            mixed = [out_a.reshape(n_tok, -1), out_b.reshape(n_tok, -1)]
            h = _out_proj(mixed, even_w_out[i].astype(BF16), post_gain[layer], h, tm=tm)
        else:
            proj = _proj(h, pre_gain[layer], odd_w_in[i].astype(BF16), tm=tm, col_chunk=1024, seq=T,
                         rope=(cos_c, sin_c, 2 * C_HEADS * dk), k_scale=dk ** -0.5)
            ret = _retention(proj.reshape(B, T, -1), odd_gn_g[i])
            h = _out_proj([ret.reshape(n_tok, -1)], odd_w_out[i].astype(BF16), post_gain[layer], h, tm=tm)
    return h.reshape(B, T, D)
```

```python
import functools

import numpy as np
import jax
import jax.numpy as jnp
from jax import lax
from jax.experimental import pallas as pl
from jax.experimental.pallas import tpu as pltpu

F32 = jnp.float32
BF16 = jnp.bfloat16

GRID_W = 64
ROPE_THETA = 10000.0
NORM_EPS = 1e-6

A_HEADS = 8
A_KV_HEADS = 2
HEAD_DIM = 64
A_WIDTH = A_HEADS * HEAD_DIM
A_KV_WIDTH = A_KV_HEADS * HEAD_DIM
B_HEADS = 8
B_WIDTH = B_HEADS * HEAD_DIM
LORA = 64
LNX_EPS = 64e-5
C_HEADS = 4
GN_EPS = 1e-5

LANES = 128
MXU_DIM = 256
RWKV_CHUNK = 64
RWKV_GROUP = 4
RET_CHUNK = 256
VMEM_LIMIT = 56 * 1024 * 1024


def _cparams(sem):
    return pltpu.CompilerParams(dimension_semantics=sem, vmem_limit_bytes=VMEM_LIMIT)


def _dot(a, b):
    return jnp.dot(a, b, preferred_element_type=F32)


def _dot_nt(a, b):
    return lax.dot_general(a, b, (((1,), (1,)), ((), ())), preferred_element_type=F32)


def _dot_tn(a, b):
    return lax.dot_general(a, b, (((0,), (0,)), ((), ())), preferred_element_type=F32)


def _bmm(a, b):
    return lax.dot_general(a, b, (((2,), (1,)), ((0,), (0,))), preferred_element_type=F32)


def _bmm_nt(a, b):
    return lax.dot_general(a, b, (((2,), (2,)), ((0,), (0,))), preferred_element_type=F32)


def _bmm_tn(a, b):
    return lax.dot_general(a, b, (((1,), (1,)), ((0,), (0,))), preferred_element_type=F32)


def _sigmoid(x):
    return 1.0 / (1.0 + jnp.exp(-x))


def _silu(x):
    return x * _sigmoid(x)


def _rope_tables(T, dim):
    half = dim // 2
    t = np.arange(T)
    row = (t // GRID_W).astype(np.float32)
    col = (t % GRID_W).astype(np.float32)
    inv_freq = (ROPE_THETA ** (-np.arange(0, half, 2, dtype=np.float32) / half)).astype(np.float32)
    ar = row[:, None] * inv_freq
    ac = col[:, None] * inv_freq
    cos = np.concatenate([np.cos(ar), np.cos(ar), np.cos(ac), np.cos(ac)], axis=1)
    sin = np.concatenate([-np.sin(ar), np.sin(ar), -np.sin(ac), np.sin(ac)], axis=1)
    return jnp.asarray(cos, F32), jnp.asarray(sin, F32)


def _head_norm_rope(x, gain, cos, sin):
    lane = lax.broadcasted_iota(jnp.int32, (1, LANES), 1)
    head0 = lane < HEAD_DIM
    sq = x * x
    s0 = jnp.sum(jnp.where(head0, sq, 0.0), axis=-1, keepdims=True)
    s1 = jnp.sum(jnp.where(head0, 0.0, sq), axis=-1, keepdims=True)
    ms = jnp.where(head0, s0, s1) * (1.0 / HEAD_DIM)
    y = x * lax.rsqrt(ms + NORM_EPS) * gain
    q4 = HEAD_DIM // 4
    first = (lane % (2 * q4)) < q4
    partner = jnp.where(first, pltpu.roll(y, LANES - q4, 1), pltpu.roll(y, q4, 1))
    return y * cos + partner * sin


def _even_col_kind(c):
    a_gate = A_WIDTH + 2 * A_KV_WIDTH
    b_gate = a_gate + A_WIDTH + 3 * B_WIDTH + 4 * LORA
    if c < A_WIDTH:
        return "q"
    if c < A_WIDTH + A_KV_WIDTH:
        return "k"
    if a_gate <= c < a_gate + A_WIDTH or c >= b_gate:
        return "silu"
    return "plain"


def _odd_col_kind(c, n_out):
    if c < n_out // 6:
        return "rope_q"
    if c < n_out // 3:
        return "rope_k"
    if c >= 2 * n_out // 3:
        return "silu"
    return "plain"


def _proj_kernel(x_ref, g_ref, w_ref, cos_ref, sin_ref, *rest, col_chunk, even, k_scale):
    if even:
        qg_ref, kg_ref, o_ref = rest
    else:
        (o_ref,) = rest
    x = x_ref[...]
    ms = jnp.mean(x * x, axis=-1, keepdims=True)
    hn = (x * lax.rsqrt(ms + NORM_EPS) * g_ref[...]).astype(BF16)
    n_out = o_ref.shape[-1]
    q_scale = HEAD_DIM ** -0.5 * np.log2(np.e)
    for c0 in range(0, n_out, col_chunk):
        acc = _dot(hn, w_ref[:, c0:c0 + col_chunk])
        for b0 in range(0, col_chunk, LANES):
            c = c0 + b0
            xb = acc[:, b0:b0 + LANES]
            kind = _even_col_kind(c) if even else _odd_col_kind(c, n_out)
            if kind == "q":
                yb = _head_norm_rope(xb, qg_ref[...], cos_ref[...], sin_ref[...]) * q_scale
            elif kind == "k":
                yb = _head_norm_rope(xb, kg_ref[...], cos_ref[...], sin_ref[...])
            elif kind in ("rope_q", "rope_k"):
                t0 = c % (2 * LANES)
                yb = xb * cos_ref[:, t0:t0 + LANES] + pltpu.roll(xb, LANES // 2, 1) * sin_ref[:, t0:t0 + LANES]
                if kind == "rope_k":
                    yb = yb * k_scale
            elif kind == "silu":
                yb = _silu(xb)
            else:
                yb = xb
            o_ref[:, c:c + LANES] = yb.astype(o_ref.dtype)


def _proj(h2, gain, w_bf, cos, sin, *, tm, col_chunk, seq, head_gains=None, k_scale=1.0):
    n_tok, d = h2.shape
    n_out = w_bf.shape[1]
    tiles_per_seq = seq // tm
    table = pl.BlockSpec((tm, cos.shape[1]), lambda i: (i % tiles_per_seq, 0))
    in_specs = [
        pl.BlockSpec((tm, d), lambda i: (i, 0)),
        pl.BlockSpec((1, d), lambda i: (0, 0)),
        pl.BlockSpec((d, n_out), lambda i: (0, 0)),
        table, table,
    ]
    args = [h2, gain.reshape(1, d), w_bf, cos, sin]
    if head_gains is not None:
        in_specs += [pl.BlockSpec((1, LANES), lambda i: (0, 0))] * 2
        args += [jnp.tile(g, LANES // HEAD_DIM).reshape(1, LANES) for g in head_gains]
    return pl.pallas_call(
        functools.partial(_proj_kernel, col_chunk=col_chunk, even=head_gains is not None, k_scale=k_scale),
        grid=(n_tok // tm,),
        in_specs=in_specs,
        out_specs=pl.BlockSpec((tm, n_out), lambda i: (i, 0)),
        out_shape=jax.ShapeDtypeStruct((n_tok, n_out), BF16),
        compiler_params=_cparams(("parallel",)),
        name="proj",
    )(*args)


def _out_kernel(*refs, n_in):
    ins = refs[:n_in]
    w_ref, g_ref, h_ref, o_ref = refs[n_in:]
    acc = None
    r0 = 0
    for m_ref in ins:
        k = m_ref.shape[-1]
        part = _dot(m_ref[...], w_ref[r0:r0 + k, :])
        acc = part if acc is None else acc + part
        r0 += k
    ms = jnp.mean(acc * acc, axis=-1, keepdims=True)
    o_ref[...] = h_ref[...] + acc * lax.rsqrt(ms + NORM_EPS) * g_ref[...]


def _out_proj(mixed, w_bf, gain, h2, *, tm):
    n_tok, d = h2.shape
    k_total = w_bf.shape[0]
    in_specs = [pl.BlockSpec((tm, m.shape[1]), lambda i: (i, 0)) for m in mixed]
    in_specs += [
        pl.BlockSpec((k_total, d), lambda i: (0, 0)),
        pl.BlockSpec((1, d), lambda i: (0, 0)),
        pl.BlockSpec((tm, d), lambda i: (i, 0)),
    ]
    return pl.pallas_call(
        functools.partial(_out_kernel, n_in=len(mixed)),
        grid=(n_tok // tm,),
        in_specs=in_specs,
        out_specs=pl.BlockSpec((tm, d), lambda i: (i, 0)),
        out_shape=jax.ShapeDtypeStruct((n_tok, d), F32),
        compiler_params=_cparams(("parallel",)),
        name="out_proj",
    )(*mixed, w_bf, gain.reshape(1, d), h2)


def _attn_kernel(q_ref, kv_ref, ga_ref, gb_ref, o_ref, kn_ref, vx_ref, *, tq):
    t_idx = pl.program_id(1)
    lane = lax.broadcasted_iota(jnp.int32, (1, LANES), 1)
    head0 = lane < HEAD_DIM

    @pl.when(t_idx == 0)
    def _():
        kn_ref[...] = kv_ref[0, :, 0:LANES].astype(F32).T.astype(BF16)
        vx_ref[:, 0:LANES] = kv_ref[0, :, LANES:2 * LANES]
        vx_ref[:, LANES:2 * LANES] = jnp.ones((kv_ref.shape[1], LANES), BF16)

    row0 = pl.multiple_of(t_idx * tq, tq)
    kn = kn_ref[...]
    vx = vx_ref[...]
    rep = A_HEADS // A_KV_HEADS
    for pair in range(A_HEADS // 2):
        g = (2 * pair) // rep
        qn = q_ref[0, pl.ds(row0, tq), pair * LANES:(pair + 1) * LANES].astype(F32)
        qsw = pltpu.roll(qn, HEAD_DIM, 1)
        grp = head0 if g == 0 else jnp.logical_not(head0)
        outs = []
        for sub in range(2):
            src = qn if sub == g else qsw
            qz = jnp.where(grp, src, 0.0).astype(BF16)
            s = _dot(qz, kn)
            m = jnp.max(s, axis=-1, keepdims=True)
            p = jnp.exp2(s - m)
            ox = _dot(p.astype(BF16), vx)
            o = ox[:, 0:LANES] * (1.0 / ox[:, LANES:2 * LANES])
            outs.append(o if sub == g else pltpu.roll(o, HEAD_DIM, 1))
        o_pair = jnp.where(head0, outs[0], outs[1])
        gref = ga_ref if pair < 2 else gb_ref
        gc = (pair % 2) * LANES
        gate = gref[0, pl.ds(row0, tq), gc:gc + LANES].astype(F32)
        o_ref[0, :, pair * LANES:(pair + 1) * LANES] = (o_pair * gate).astype(o_ref.dtype)


def _attention(proj3, *, tq):
    B, T, _ = proj3.shape
    return pl.pallas_call(
        functools.partial(_attn_kernel, tq=tq),
        grid=(B, T // tq),
        in_specs=[
            pl.BlockSpec((1, T, A_WIDTH), lambda b, t: (b, 0, 0)),
            pl.BlockSpec((1, T, 2 * A_KV_WIDTH), lambda b, t: (b, 0, A_WIDTH // (2 * A_KV_WIDTH))),
            pl.BlockSpec((1, T, 2 * LANES), lambda b, t: (b, 0, 3)),
            pl.BlockSpec((1, T, 2 * LANES), lambda b, t: (b, 0, 4)),
        ],
        out_specs=pl.BlockSpec((1, tq, A_WIDTH), lambda b, t: (b, t, 0)),
        out_shape=jax.ShapeDtypeStruct((B, T, A_WIDTH), BF16),
        scratch_shapes=[pltpu.VMEM((LANES, T), BF16), pltpu.VMEM((T, 2 * LANES), BF16)],
        compiler_params=_cparams(("parallel", "arbitrary")),
        name="attention",
    )(proj3, proj3, proj3, proj3)


RWKV_LANES = MXU_DIM


def _rwkv_groups(T):
    n_chunks = T // RWKV_CHUNK
    cg = min(RWKV_GROUP, n_chunks // 2)
    n_groups = n_chunks // cg
    assert n_groups % 2 == 0 and n_groups * cg == n_chunks
    return n_chunks, cg, n_groups


def _rwkv_kernel(r_ref, k_ref, v_ref, lo_ref, gate_ref,
                 mur_ref, muk_ref, muv_ref, mulo_ref, wbig_ref,
                 w0f_ref, a0f_ref, w0b_ref, a0b_ref,
                 kk_ref, ka_ref, rk_ref, lng_ref, lnb_ref,
                 o_ref,
                 r_s, v_s, a_s, k_s, b_s, lw_s, y_s, st_s, ss_s,
                 mt_a, gt_a, rp_a, y1_a, dec_a, mt_b, gt_b, rp_b, y1_b, dec_b):
    T = r_ref.shape[1]
    L = RWKV_CHUNK
    HW = RWKV_LANES
    assert L == HEAD_DIM
    n_chunks, CG, n_groups = _rwkv_groups(T)
    GT = CG * L
    row = lax.broadcasted_iota(jnp.int32, (T, 1), 0)
    lane = lax.broadcasted_iota(jnp.int32, (1, LANES), 1)
    head0 = lane < HEAD_DIM

    def shift(ref, mu_ref):
        u = ref[0].astype(F32)
        prev = jnp.where(row == 0, 0.0, pltpu.roll(u, 1, 0))
        nxt = jnp.where(row == T - 1, 0.0, pltpu.roll(u, T - 1, 0))
        return u + mu_ref[...] * (0.5 * (prev + nxt) - u)

    def head_sum(x):
        parts = []
        for c0 in range(0, HW, LANES):
            xc = x[:, c0:c0 + LANES]
            s0 = jnp.sum(jnp.where(head0, xc, 0.0), axis=-1, keepdims=True)
            s1 = jnp.sum(jnp.where(head0, 0.0, xc), axis=-1, keepdims=True)
            parts.append(jnp.where(head0, s0, s1))
        return jnp.concatenate(parts, axis=1)

    r = shift(r_ref, mur_ref)
    kb = shift(k_ref, muk_ref)
    vb = shift(v_ref, muv_ref)
    lo = shift(lo_ref, mulo_ref)
    lo_t = jnp.concatenate([jnp.tanh(lo[:, :2 * LORA]), lo[:, 2 * LORA:]], axis=1).astype(BF16)
    z = _dot(lo_t, wbig_ref[0])

    kkv = kb * kk_ref[...]
    kkn = kkv / jnp.maximum(jnp.sqrt(head_sum(kkv * kkv)), 1e-12)
    bonus = head_sum(r * kb * rk_ref[...]) * vb

    r_s[...] = r.astype(BF16)
    v_s[...] = vb.astype(BF16)
    a_s[...] = (-kkn).astype(BF16)
    for d, (w0_ref, a0_ref) in enumerate(((w0f_ref, a0f_ref), (w0b_ref, a0b_ref))):
        xw = w0_ref[...] + z[:, d * HW:(d + 1) * HW]
        lw_s[d] = (-np.exp(-0.5)) * _sigmoid(xw)
        iclr = _sigmoid(a0_ref[...] + z[:, (2 + d) * HW:(3 + d) * HW])
        k_s[d] = (kb * (1.0 + (iclr - 1.0) * ka_ref[...])).astype(BF16)
        b_s[d] = (kkn * iclr).astype(BF16)

    pos = lax.broadcasted_iota(jnp.int32, (GT, 1), 0) % L
    trow = lax.broadcasted_iota(jnp.int32, (L, HW), 0)
    scol = lax.broadcasted_iota(jnp.int32, (L, HW), 1) % L
    eye = (trow == scol).astype(F32)
    blk_r = lax.broadcasted_iota(jnp.int32, (HW, HW), 0) // L
    blk_c = lax.broadcasted_iota(jnp.int32, (HW, HW), 1) // HEAD_DIM
    bdm = jnp.where(blk_r == blk_c, 1.0, 0.0).astype(BF16)

    def bd(x):
        return jnp.concatenate([x] * (HW // L), axis=1) * bdm

    def operands(d, gi):
        reverse = d == 1
        sl = pl.ds(pl.multiple_of(gi * GT, GT), GT)
        lw = lw_s[d, sl, :]
        cf = lw
        sh = 1
        while sh < L:
            cf = cf + jnp.where(pos >= sh, pltpu.roll(cf, sh, 0), 0.0)
            sh *= 2
        to3 = lambda x: x.reshape(CG, L, HW)
        cf3, lw3 = to3(cf), to3(lw)
        tot = cf3[:, L - 1:L, :]
        if reverse:
            cin = tot - cf3 + lw3
            cex = tot - cf3
        else:
            cin = cf3
            cex = cf3 - lw3
        a3, r3 = to3(a_s[sl, :].astype(F32)), to3(r_s[sl, :].astype(F32))
        k3, b3 = to3(k_s[d, sl, :].astype(F32)), to3(b_s[d, sl, :].astype(F32))
        v_bf = to3(v_s[sl, :])
        ek = jnp.exp(-cin)
        ekt = jnp.exp(tot - cin)
        return dict(
            at=(a3 * jnp.exp(cex)).astype(BF16), rh=r3 * jnp.exp(cin),
            bh=(b3 * ek).astype(BF16), kh=(k3 * ek).astype(BF16),
            bt=(b3 * ekt).astype(BF16), kt=(k3 * ekt).astype(BF16),
            v=v_bf, dec=jnp.exp(tot))

    def tri(x, fwd_mask, bwd_mask):
        return jnp.concatenate([jnp.where(fwd_mask, x[:CG], 0.0), jnp.where(bwd_mask, x[CG:], 0.0)], axis=0)

    def precompute(gf, gb, mt_o, gt_o, rp_o, y1_o, dec_o):
        of, ob = operands(0, gf), operands(1, gb)
        cat = lambda name: jnp.concatenate([of[name], ob[name]], axis=0)
        at, rh, bt, kt, v_bf = cat("at"), cat("rh"), cat("bt"), cat("kt"), cat("v")
        vs = bd(v_bf)
        strict = lambda x: tri(x, trow > scol, trow < scol)
        incl = lambda x: tri(x, trow >= scol, trow <= scol)

        lhs = jnp.concatenate([at, rh.astype(BF16)], axis=1)
        sb = _bmm_nt(lhs, bd(cat("bh")))
        sk = _bmm_nt(lhs, bd(cat("kh")))
        aab = strict(sb[:, :L])
        arb = incl(sb[:, L:]).astype(BF16)
        aak = strict(sk[:, :L]).astype(BF16)
        ark = incl(sk[:, L:]).astype(BF16)

        tinv = eye + aab
        aab_bf = aab.astype(BF16)
        pw = _bmm(aab_bf, bd(aab_bf))
        step = 4
        while step < L:
            both = _bmm(jnp.concatenate([tinv.astype(BF16), pw.astype(BF16)], axis=1), bd(pw.astype(BF16)))
            tinv = tinv + both[:, :L]
            pw = both[:, L:]
            step *= 2
        tinv = (tinv + _bmm(tinv.astype(BF16), bd(pw.astype(BF16)))).astype(BF16)

        akv = _bmm(jnp.concatenate([aak, ark], axis=1), vs)
        ap = _bmm(tinv, bd(at)).astype(BF16)
        u0 = _bmm(tinv, bd(akv[:, :L].astype(BF16))).astype(BF16)
        split = lambda x: x.reshape((2, CG) + x.shape[1:])
        rp_o[...] = split((rh + _bmm(arb, bd(ap))).astype(BF16))
        y1_o[...] = split(_bmm(arb, bd(u0)) + akv[:, L:])
        mt_o[...] = split(_bmm_tn(ap, bt).astype(BF16))
        gt_o[...] = split(_bmm_tn(jnp.concatenate([u0, v_bf], axis=1),
                                  jnp.concatenate([bt, kt], axis=1)))
        dec_o[...] = split(cat("dec"))

    def recur_readout(gf, gb, mt_i, gt_i, rp_i, y1_i, dec_i):
        for i in range(CG):
            for d, j in ((0, i), (1, CG - 1 - i)):
                s = st_s[d]
                s_bf = s.astype(BF16) * bdm
                ss_s[d, j] = s_bf
                st_s[d] = s * dec_i[d, j] + _dot(s_bf, mt_i[d, j]) + gt_i[d, j]
        for d, g in ((0, gf), (1, gb)):
            y = _bmm_nt(rp_i[d], ss_s[d]) + y1_i[d]
            rows = pl.ds(pl.multiple_of((g + 1) * GT, GT), GT)
            y_s[rows, :] += y.reshape(GT, HW)

    buf_a = (mt_a, gt_a, rp_a, y1_a, dec_a)
    buf_b = (mt_b, gt_b, rp_b, y1_b, dec_b)
    for ref in buf_b + (st_s, y_s):
        ref[...] = jnp.zeros_like(ref)

    def pipelined(j, carry):
        g = 2 * j
        precompute(g, n_groups - 1 - g, *buf_a)
        recur_readout(g - 1, n_groups - g, *buf_b)
        precompute(g + 1, n_groups - 2 - g, *buf_b)
        recur_readout(g, n_groups - 1 - g, *buf_a)
        return carry

    lax.fori_loop(0, n_groups // 2, pipelined, 0)
    recur_readout(n_groups - 1, 0, *buf_b)

    y = y_s[GT:GT + T, :]
    mean = head_sum(y) * (1.0 / HEAD_DIM)
    yc = y - mean
    var = head_sum(yc * yc) * (1.0 / HEAD_DIM)
    yn = yc * lax.rsqrt(var + LNX_EPS) * lng_ref[...] + lnb_ref[...]
    o_ref[0] = ((yn + bonus) * gate_ref[0].astype(F32)).astype(o_ref.dtype)


def _rwkv(proj3, p):
    B, T, _ = proj3.shape
    HW = RWKV_LANES
    L = RWKV_CHUNK
    n_q = B_WIDTH // HW
    base = (A_WIDTH + 2 * A_KV_WIDTH + A_WIDTH) // HW
    _, cg, n_groups = _rwkv_groups(T)
    vec = lambda a: a.reshape(1, -1)
    blk = lambda off: pl.BlockSpec((1, T, HW), lambda b, h: (b, 0, off + h))
    pvec = lambda off=0: pl.BlockSpec((1, HW), lambda b, h: (0, off + h))
    mu = vec(p["mu"])
    in_specs = [
        blk(base), blk(base + n_q), blk(base + 2 * n_q),
        pl.BlockSpec((1, T, 4 * LORA), lambda b, h: (b, 0, base + 3 * n_q)),
        blk(base + 3 * n_q + 1),
        pvec(0), pvec(n_q), pvec(2 * n_q),
        pl.BlockSpec((1, 4 * LORA), lambda b, h: (0, 3 * n_q)),
        pl.BlockSpec((1, 4 * LORA, 4 * HW), lambda b, h: (h, 0, 0)),
    ] + [pvec()] * 9
    group_bufs = [
        pltpu.VMEM((2, cg, HW, HW), BF16),
        pltpu.VMEM((2, cg, HW, HW), F32),
        pltpu.VMEM((2, cg, L, HW), BF16),
        pltpu.VMEM((2, cg, L, HW), F32),
        pltpu.VMEM((2, cg, 1, HW), F32),
    ]
    scratch = [
        pltpu.VMEM((T, HW), BF16),
        pltpu.VMEM((T, HW), BF16),
        pltpu.VMEM((T, HW), BF16),
        pltpu.VMEM((2, T, HW), BF16),
        pltpu.VMEM((2, T, HW), BF16),
        pltpu.VMEM((2, T, HW), F32),
        pltpu.VMEM(((n_groups + 2) * cg * L, HW), F32),
        pltpu.VMEM((2, HW, HW), F32),
        pltpu.VMEM((2, cg, HW, HW), BF16),
    ] + group_bufs + group_bufs
    return pl.pallas_call(
        _rwkv_kernel,
        grid=(B, n_q),
        in_specs=in_specs,
        out_specs=pl.BlockSpec((1, T, HW), lambda b, h: (b, 0, h)),
        out_shape=jax.ShapeDtypeStruct((B, T, B_WIDTH), BF16),
        scratch_shapes=scratch,
        compiler_params=_cparams(("parallel", "parallel")),
        name="rwkv7",
    )(proj3, proj3, proj3, proj3, proj3, mu, mu, mu, mu, p["wbig"],
      vec(p["w0_f"]), vec(p["a0_f"]), vec(p["w0_b"]), vec(p["a0_b"]),
      vec(p["k_k"]), vec(p["k_a"]), vec(p["r_k"]), vec(p["lnx_g"]), vec(p["lnx_b"]))


def _lora_weights(w2_f, w2_b, a2_f, a2_b):
    HW = RWKV_LANES
    n_q = B_WIDTH // HW
    out = jnp.zeros((n_q, 4 * LORA, 4 * HW), F32)
    for i, w in enumerate((w2_f, w2_b, a2_f, a2_b)):
        wp = w.reshape(LORA, n_q, HW).transpose(1, 0, 2)
        out = out.at[:, i * LORA:(i + 1) * LORA, i * HW:(i + 1) * HW].set(wp)
    return out.astype(BF16)


def _ret_kernel(lg_ref, q_ref, k_ref, v_ref, g_ref, gn_ref, o_ref, st_s, sb_s):
    h = pl.program_id(1)
    T = q_ref.shape[1]
    C = min(RET_CHUNK, T)
    n = T // C
    lgf = lg_ref[h, 0]
    lgb = lg_ref[h, 1]
    idx = lax.broadcasted_iota(jnp.int32, (C, 1), 0).astype(F32)
    ji = lax.broadcasted_iota(jnp.int32, (C, C), 0)
    li = lax.broadcasted_iota(jnp.int32, (C, C), 1)
    diff = (ji - li).astype(F32)
    dmat = jnp.exp(jnp.where(ji >= li, diff * lgf, -diff * lgb))
    dk = q_ref.shape[-1]
    wide = lambda col: jnp.broadcast_to(col, (C, dk)).astype(BF16)
    qdec_f = wide(jnp.exp((idx + 1.0) * lgf))
    kdec_f = wide(jnp.exp((C - 1.0 - idx) * lgf))
    qdec_b = wide(jnp.exp((C - 1.0 - idx) * lgb))
    kdec_b = wide(jnp.exp((idx + 1.0) * lgb))
    cdec_f = jnp.exp(jnp.full((1, 1), float(C), F32) * lgf)
    cdec_b = jnp.exp(jnp.full((1, 1), float(C), F32) * lgb)

    def chunk(i):
        sl = pl.ds(i * C, C)
        return sl, q_ref[0, sl, :], k_ref[0, sl, :], v_ref[0, sl, :]

    st_s[...] = jnp.zeros_like(st_s)
    for c in range(n - 1, -1, -1):
        st = st_s[...]
        sb_s[c] = st.astype(BF16)
        if c > 0:
            _, _, k, v = chunk(c)
            st_s[...] = st * cdec_b + _dot_tn(k * kdec_b, v)

    st_s[...] = jnp.zeros_like(st_s)
    inv_n = 1.0 / o_ref.shape[-1]
    for c in range(n):
        sl, q, k, v = chunk(c)
        st = st_s[...]
        s = (_dot_nt(q, k) * dmat).astype(BF16)
        lhs = jnp.concatenate([s, q * qdec_f, q * qdec_b], axis=1)
        rhs = jnp.concatenate([v, st.astype(BF16), sb_s[c]], axis=0)
        o = _dot(lhs, rhs)
        if c < n - 1:
            st_s[...] = st * cdec_f + _dot_tn(k * kdec_f, v)
        mean = jnp.sum(o, axis=-1, keepdims=True) * inv_n
        oc = o - mean
        var = jnp.sum(oc * oc, axis=-1, keepdims=True) * inv_n
        y = oc * lax.rsqrt(var + GN_EPS) * gn_ref[...]
        o_ref[0, sl, :] = (g_ref[0, sl, :].astype(F32) * y).astype(o_ref.dtype)


def _retention(proj3, gn_g):
    B, T, width = proj3.shape
    dk = (width // 6) // C_HEADS
    dv = 2 * dk
    lg_f = np.log(1.0 - 2.0 ** (-5.0 - np.arange(C_HEADS, dtype=np.float32))).astype(np.float32)
    lg = jnp.asarray(np.stack([lg_f, lg_f[::-1]], axis=1), F32)
    return pl.pallas_call(
        _ret_kernel,
        grid=(B, C_HEADS),
        in_specs=[
            pl.BlockSpec(memory_space=pltpu.SMEM),
            pl.BlockSpec((1, T, dk), lambda b, h: (b, 0, h)),
            pl.BlockSpec((1, T, dk), lambda b, h: (b, 0, C_HEADS + h)),
            pl.BlockSpec((1, T, dv), lambda b, h: (b, 0, C_HEADS + h)),
            pl.BlockSpec((1, T, dv), lambda b, h: (b, 0, 2 * C_HEADS + h)),
            pl.BlockSpec((1, dv), lambda b, h: (0, h)),
        ],
        out_specs=pl.BlockSpec((1, T, dv), lambda b, h: (b, 0, h)),
        out_shape=jax.ShapeDtypeStruct((B, T, C_HEADS * dv), BF16),
        scratch_shapes=[pltpu.VMEM((dk, dv), F32), pltpu.VMEM((T // min(RET_CHUNK, T), dk, dv), BF16)],
        compiler_params=_cparams(("parallel", "parallel")),
        name="retention",
    )(lg, proj3, proj3, proj3, proj3, gn_g.reshape(1, -1))


def kernel(x, pre_gain, post_gain, even_w_in, even_mu, even_q_gain, even_k_gain, even_k_k, even_k_a, even_r_k, even_w0_f, even_w2_f, even_a0_f, even_a2_f, even_w0_b, even_w2_b, even_a0_b, even_a2_b, even_lnx_g, even_lnx_b, even_w_out, odd_w_in, odd_gn_g, odd_w_out):
    B, T, D = x.shape
    n_tok = B * T
    depth = pre_gain.shape[0]
    tm = min(512, T)
    tq = min(512, T)
    cos_a, sin_a = _rope_tables(T, HEAD_DIM)
    cos_a, sin_a = jnp.tile(cos_a, (1, 2)), jnp.tile(sin_a, (1, 2))
    dk = odd_w_in.shape[-1] // 6 // C_HEADS
    cos_c, sin_c = _rope_tables(T, dk)

    h = x.reshape(n_tok, D)
    for layer in range(depth):
        i = layer // 2
        if layer % 2 == 0:
            proj = _proj(h, pre_gain[layer], even_w_in[i].astype(BF16), cos_a, sin_a, tm=tm, col_chunk=MXU_DIM,
                         seq=T, head_gains=(even_q_gain[i], even_k_gain[i]))
            proj3 = proj.reshape(B, T, -1)
            out_a = _attention(proj3, tq=tq)
            params = dict(
                mu=even_mu[i], k_k=even_k_k[i], k_a=even_k_a[i], r_k=even_r_k[i],
                w0_f=even_w0_f[i], a0_f=even_a0_f[i], w0_b=even_w0_b[i], a0_b=even_a0_b[i],
                lnx_g=even_lnx_g[i], lnx_b=even_lnx_b[i],
                wbig=_lora_weights(even_w2_f[i], even_w2_b[i], even_a2_f[i], even_a2_b[i]),
            )
            out_b = _rwkv(proj3, params)
            mixed = [out_a.reshape(n_tok, -1), out_b.reshape(n_tok, -1)]
            h = _out_proj(mixed, even_w_out[i].astype(BF16), post_gain[layer], h, tm=tm)
        else:
            proj = _proj(h, pre_gain[layer], odd_w_in[i].astype(BF16), cos_c, sin_c, tm=tm, col_chunk=1024,
                         seq=T, k_scale=dk ** -0.5)
            ret = _retention(proj.reshape(B, T, -1), odd_gn_g[i])
            h = _out_proj([ret.reshape(n_tok, -1)], odd_w_out[i].astype(BF16), post_gain[layer], h, tm=tm)
    return h.reshape(B, T, D)
```

```python
import functools

import numpy as np
import jax
import jax.numpy as jnp
from jax import lax
from jax.experimental import pallas as pl
from jax.experimental.pallas import tpu as pltpu

F32 = jnp.float32
BF16 = jnp.bfloat16

GRID_W = 64
ROPE_THETA = 10000.0
NORM_EPS = 1e-6

A_HEADS = 8
A_KV_HEADS = 2
HEAD_DIM = 64
A_WIDTH = A_HEADS * HEAD_DIM
A_KV_WIDTH = A_KV_HEADS * HEAD_DIM
B_HEADS = 8
B_WIDTH = B_HEADS * HEAD_DIM
LORA = 64
LNX_EPS = 64e-5
C_HEADS = 4
GN_EPS = 1e-5

LANES = 128
MXU_DIM = 256
RWKV_CHUNK = 64
RWKV_GROUP = 4
RET_CHUNK = 256
VMEM_LIMIT = 56 * 1024 * 1024


def _cparams(sem):
    return pltpu.CompilerParams(dimension_semantics=sem, vmem_limit_bytes=VMEM_LIMIT)


def _dot(a, b):
    return jnp.dot(a, b, preferred_element_type=F32)


def _dot_nt(a, b):
    return lax.dot_general(a, b, (((1,), (1,)), ((), ())), preferred_element_type=F32)


def _dot_tn(a, b):
    return lax.dot_general(a, b, (((0,), (0,)), ((), ())), preferred_element_type=F32)


def _bmm(a, b):
    return lax.dot_general(a, b, (((2,), (1,)), ((0,), (0,))), preferred_element_type=F32)


def _bmm_nt(a, b):
    return lax.dot_general(a, b, (((2,), (2,)), ((0,), (0,))), preferred_element_type=F32)


def _bmm_tn(a, b):
    return lax.dot_general(a, b, (((1,), (1,)), ((0,), (0,))), preferred_element_type=F32)


def _sigmoid(x):
    return 1.0 / (1.0 + jnp.exp(-x))


def _silu(x):
    return x * _sigmoid(x)


def _rope_tables(T, dim):
    half = dim // 2
    t = np.arange(T)
    row = (t // GRID_W).astype(np.float32)
    col = (t % GRID_W).astype(np.float32)
    inv_freq = (ROPE_THETA ** (-np.arange(0, half, 2, dtype=np.float32) / half)).astype(np.float32)
    ar = row[:, None] * inv_freq
    ac = col[:, None] * inv_freq
    cos = np.concatenate([np.cos(ar), np.cos(ar), np.cos(ac), np.cos(ac)], axis=1)
    sin = np.concatenate([-np.sin(ar), np.sin(ar), -np.sin(ac), np.sin(ac)], axis=1)
    return jnp.asarray(cos, F32), jnp.asarray(sin, F32)


def _head_norm_rope(x, gain, cos, sin):
    lane = lax.broadcasted_iota(jnp.int32, (1, LANES), 1)
    head0 = lane < HEAD_DIM
    sq = x * x
    s0 = jnp.sum(jnp.where(head0, sq, 0.0), axis=-1, keepdims=True)
    s1 = jnp.sum(jnp.where(head0, 0.0, sq), axis=-1, keepdims=True)
    ms = jnp.where(head0, s0, s1) * (1.0 / HEAD_DIM)
    y = x * lax.rsqrt(ms + NORM_EPS) * gain
    q4 = HEAD_DIM // 4
    first = (lane % (2 * q4)) < q4
    partner = jnp.where(first, pltpu.roll(y, LANES - q4, 1), pltpu.roll(y, q4, 1))
    return y * cos + partner * sin


def _even_col_kind(c):
    a_gate = A_WIDTH + 2 * A_KV_WIDTH
    b_gate = a_gate + A_WIDTH + 3 * B_WIDTH + 4 * LORA
    if c < A_WIDTH:
        return "q"
    if c < A_WIDTH + A_KV_WIDTH:
        return "k"
    if a_gate <= c < a_gate + A_WIDTH or c >= b_gate:
        return "silu"
    return "plain"


def _odd_col_kind(c, n_out):
    if c < n_out // 6:
        return "rope_q"
    if c < n_out // 3:
        return "rope_k"
    if c >= 2 * n_out // 3:
        return "silu"
    return "plain"


def _proj_kernel(x_ref, g_ref, w_ref, cos_ref, sin_ref, *rest, col_chunk, even, k_scale):
    if even:
        qg_ref, kg_ref, o_ref = rest
    else:
        (o_ref,) = rest
    x = x_ref[...]
    ms = jnp.mean(x * x, axis=-1, keepdims=True)
    hn = (x * lax.rsqrt(ms + NORM_EPS) * g_ref[...]).astype(BF16)
    n_out = o_ref.shape[-1]
    q_scale = HEAD_DIM ** -0.5 * np.log2(np.e)
    for c0 in range(0, n_out, col_chunk):
        acc = _dot(hn, w_ref[:, c0:c0 + col_chunk])
        for b0 in range(0, col_chunk, LANES):
            c = c0 + b0
            xb = acc[:, b0:b0 + LANES]
            kind = _even_col_kind(c) if even else _odd_col_kind(c, n_out)
            if kind == "q":
                yb = _head_norm_rope(xb, qg_ref[...], cos_ref[...], sin_ref[...]) * q_scale
            elif kind == "k":
                yb = _head_norm_rope(xb, kg_ref[...], cos_ref[...], sin_ref[...])
            elif kind in ("rope_q", "rope_k"):
                t0 = c % (2 * LANES)
                yb = xb * cos_ref[:, t0:t0 + LANES] + pltpu.roll(xb, LANES // 2, 1) * sin_ref[:, t0:t0 + LANES]
                if kind == "rope_k":
                    yb = yb * k_scale
            elif kind == "silu":
                yb = _silu(xb)
            else:
                yb = xb
            o_ref[:, c:c + LANES] = yb.astype(o_ref.dtype)


def _proj(h2, gain, w_bf, cos, sin, *, tm, col_chunk, seq, head_gains=None, k_scale=1.0):
    n_tok, d = h2.shape
    n_out = w_bf.shape[1]
    tiles_per_seq = seq // tm
    table = pl.BlockSpec((tm, cos.shape[1]), lambda i: (i % tiles_per_seq, 0))
    in_specs = [
        pl.BlockSpec((tm, d), lambda i: (i, 0)),
        pl.BlockSpec((1, d), lambda i: (0, 0)),
        pl.BlockSpec((d, n_out), lambda i: (0, 0)),
        table, table,
    ]
    args = [h2, gain.reshape(1, d), w_bf, cos, sin]
    if head_gains is not None:
        in_specs += [pl.BlockSpec((1, LANES), lambda i: (0, 0))] * 2
        args += [jnp.tile(g, LANES // HEAD_DIM).reshape(1, LANES) for g in head_gains]
    return pl.pallas_call(
        functools.partial(_proj_kernel, col_chunk=col_chunk, even=head_gains is not None, k_scale=k_scale),
        grid=(n_tok // tm,),
        in_specs=in_specs,
        out_specs=pl.BlockSpec((tm, n_out), lambda i: (i, 0)),
        out_shape=jax.ShapeDtypeStruct((n_tok, n_out), BF16),
        compiler_params=_cparams(("parallel",)),
        name="proj",
    )(*args)


def _out_kernel(*refs, n_in):
    ins = refs[:n_in]
    w_ref, g_ref, h_ref, o_ref = refs[n_in:]
    acc = None
    r0 = 0
    for m_ref in ins:
        k = m_ref.shape[-1]
        part = _dot(m_ref[...], w_ref[r0:r0 + k, :])
        acc = part if acc is None else acc + part
        r0 += k
    ms = jnp.mean(acc * acc, axis=-1, keepdims=True)
    o_ref[...] = h_ref[...] + acc * lax.rsqrt(ms + NORM_EPS) * g_ref[...]


def _out_proj(mixed, w_bf, gain, h2, *, tm):
    n_tok, d = h2.shape
    k_total = w_bf.shape[0]
    in_specs = [pl.BlockSpec((tm, m.shape[1]), lambda i: (i, 0)) for m in mixed]
    in_specs += [
        pl.BlockSpec((k_total, d), lambda i: (0, 0)),
        pl.BlockSpec((1, d), lambda i: (0, 0)),
        pl.BlockSpec((tm, d), lambda i: (i, 0)),
    ]
    return pl.pallas_call(
        functools.partial(_out_kernel, n_in=len(mixed)),
        grid=(n_tok // tm,),
        in_specs=in_specs,
        out_specs=pl.BlockSpec((tm, d), lambda i: (i, 0)),
        out_shape=jax.ShapeDtypeStruct((n_tok, d), F32),
        compiler_params=_cparams(("parallel",)),
        name="out_proj",
    )(*mixed, w_bf, gain.reshape(1, d), h2)


ONES_ROWS = 16


def _attn_kernel(q_ref, kv_ref, ga_ref, gb_ref, o_ref, vt_ref, *, tq):
    t_idx = pl.program_id(1)
    T = kv_ref.shape[1]
    lane = lax.broadcasted_iota(jnp.int32, (1, LANES), 1)
    head0 = lane < HEAD_DIM

    @pl.when(t_idx == 0)
    def _():
        vt = kv_ref[0, :, LANES:2 * LANES].astype(F32).T
        for g in range(A_KV_HEADS):
            vt_ref[g, 0:HEAD_DIM, :] = vt[g * HEAD_DIM:(g + 1) * HEAD_DIM].astype(BF16)
            vt_ref[g, HEAD_DIM:HEAD_DIM + ONES_ROWS, :] = jnp.ones((ONES_ROWS, T), BF16)

    row0 = pl.multiple_of(t_idx * tq, tq)
    kn = kv_ref[0, :, 0:LANES]
    rep = A_HEADS // A_KV_HEADS

    def scores(head):
        pair, sub, g = head // 2, head % 2, head // rep
        qn = q_ref[0, pl.ds(row0, tq), pair * LANES:(pair + 1) * LANES].astype(F32)
        src = qn if sub == g else pltpu.roll(qn, HEAD_DIM, 1)
        grp = head0 if g == 0 else jnp.logical_not(head0)
        return _dot_nt(kn, jnp.where(grp, src, 0.0).astype(BF16))

    def values(head, st):
        p = jnp.exp2(st - jnp.max(st, axis=0, keepdims=True)).astype(BF16)
        ox = _dot(vt_ref[head // rep], p)
        return ox[0:HEAD_DIM] * (1.0 / ox[HEAD_DIM:HEAD_DIM + 1])

    outs = []
    st = scores(0)
    for head in range(A_HEADS):
        st_next = scores(head + 1) if head + 1 < A_HEADS else None
        outs.append(values(head, st))
        st = st_next
        if head % 2 == 1:
            pair = head // 2
            o_pair = jnp.concatenate(outs, axis=0).T
            outs = []
            gref = ga_ref if pair < 2 else gb_ref
            gc = (pair % 2) * LANES
            gate = gref[0, pl.ds(row0, tq), gc:gc + LANES].astype(F32)
            o_ref[0, :, pair * LANES:(pair + 1) * LANES] = (o_pair * gate).astype(o_ref.dtype)


def _attention(proj3, *, tq):
    B, T, _ = proj3.shape
    return pl.pallas_call(
        functools.partial(_attn_kernel, tq=tq),
        grid=(B, T // tq),
        in_specs=[
            pl.BlockSpec((1, T, A_WIDTH), lambda b, t: (b, 0, 0)),
            pl.BlockSpec((1, T, 2 * A_KV_WIDTH), lambda b, t: (b, 0, A_WIDTH // (2 * A_KV_WIDTH))),
            pl.BlockSpec((1, T, 2 * LANES), lambda b, t: (b, 0, 3)),
            pl.BlockSpec((1, T, 2 * LANES), lambda b, t: (b, 0, 4)),
        ],
        out_specs=pl.BlockSpec((1, tq, A_WIDTH), lambda b, t: (b, t, 0)),
        out_shape=jax.ShapeDtypeStruct((B, T, A_WIDTH), BF16),
        scratch_shapes=[pltpu.VMEM((A_KV_HEADS, HEAD_DIM + ONES_ROWS, T), BF16)],
        compiler_params=_cparams(("parallel", "arbitrary")),
        name="attention",
    )(proj3, proj3, proj3, proj3)


RWKV_LANES = MXU_DIM


def _rwkv_groups(T):
    n_chunks = T // RWKV_CHUNK
    cg = min(RWKV_GROUP, n_chunks // 2)
    n_groups = n_chunks // cg
    assert n_groups % 2 == 0 and n_groups * cg == n_chunks
    return n_chunks, cg, n_groups


def _rwkv_kernel(r_ref, k_ref, v_ref, lo_ref, gate_ref,
                 mur_ref, muk_ref, muv_ref, mulo_ref, wbig_ref,
                 w0f_ref, a0f_ref, w0b_ref, a0b_ref,
                 kk_ref, ka_ref, rk_ref, lng_ref, lnb_ref,
                 o_ref,
                 r_s, v_s, a_s, k_s, b_s, lw_s, y_s, st_s, ss_s,
                 mt_a, gt_a, rp_a, y1_a, dec_a, mt_b, gt_b, rp_b, y1_b, dec_b):
    T = r_ref.shape[1]
    L = RWKV_CHUNK
    HW = RWKV_LANES
    assert L == HEAD_DIM
    n_chunks, CG, n_groups = _rwkv_groups(T)
    GT = CG * L
    lane = lax.broadcasted_iota(jnp.int32, (1, LANES), 1)
    head0 = lane < HEAD_DIM

    SHIFT_ROWS = LANES
    n_shift = T // SHIFT_ROWS
    win_start = [min(max(i * SHIFT_ROWS - (MXU_DIM - SHIFT_ROWS) // 2, 0), T - MXU_DIM) for i in range(n_shift)]
    t_in = lax.broadcasted_iota(jnp.int32, (SHIFT_ROWS, MXU_DIM), 0)
    j_in = lax.broadcasted_iota(jnp.int32, (SHIFT_ROWS, MXU_DIM), 1)
    pick = {off: jnp.where((j_in == t_in + off - 1) | (j_in == t_in + off + 1), 1.0, 0.0).astype(BF16)
            for off in sorted({i * SHIFT_ROWS - s for i, s in enumerate(win_start)})}

    def shift(ref, mu_ref):
        u = ref[0].astype(F32)
        both = jnp.concatenate(
            [_dot(pick[i * SHIFT_ROWS - s], ref[0, s:s + MXU_DIM, :]) for i, s in enumerate(win_start)], axis=0)
        return u + mu_ref[...] * (0.5 * both - u)

    def head_sum(x):
        parts = []
        for c0 in range(0, HW, LANES):
            xc = x[:, c0:c0 + LANES]
            s0 = jnp.sum(jnp.where(head0, xc, 0.0), axis=-1, keepdims=True)
            s1 = jnp.sum(jnp.where(head0, 0.0, xc), axis=-1, keepdims=True)
            parts.append(jnp.where(head0, s0, s1))
        return jnp.concatenate(parts, axis=1)

    r = shift(r_ref, mur_ref)
    kb = shift(k_ref, muk_ref)
    vb = shift(v_ref, muv_ref)
    lo = shift(lo_ref, mulo_ref)
    lo_t = jnp.concatenate([jnp.tanh(lo[:, :2 * LORA]), lo[:, 2 * LORA:]], axis=1).astype(BF16)
    z = _dot(lo_t, wbig_ref[0])

    kkv = kb * kk_ref[...]
    kkn = kkv * lax.rsqrt(jnp.maximum(head_sum(kkv * kkv), 1e-24))
    bonus = head_sum(r * kb * rk_ref[...]) * vb

    r_s[...] = r.astype(BF16)
    v_s[...] = vb.astype(BF16)
    a_s[...] = (-kkn).astype(BF16)
    for d, (w0_ref, a0_ref) in enumerate(((w0f_ref, a0f_ref), (w0b_ref, a0b_ref))):
        xw = w0_ref[...] + z[:, d * HW:(d + 1) * HW]
        lw_s[d] = (-np.exp(-0.5)) * _sigmoid(xw)
        iclr = _sigmoid(a0_ref[...] + z[:, (2 + d) * HW:(3 + d) * HW])
        k_s[d] = (kb * (1.0 + (iclr - 1.0) * ka_ref[...])).astype(BF16)
        b_s[d] = (kkn * iclr).astype(BF16)

    pos = lax.broadcasted_iota(jnp.int32, (GT, 1), 0) % L
    trow = lax.broadcasted_iota(jnp.int32, (L, HW), 0)
    scol = lax.broadcasted_iota(jnp.int32, (L, HW), 1) % L
    eye = (trow == scol).astype(F32)
    blk_r = lax.broadcasted_iota(jnp.int32, (HW, HW), 0) // L
    blk_c = lax.broadcasted_iota(jnp.int32, (HW, HW), 1) // HEAD_DIM
    bdm = jnp.where(blk_r == blk_c, 1.0, 0.0).astype(BF16)

    def bd(x):
        return jnp.concatenate([x] * (HW // L), axis=1) * bdm

    def operands(d, gi):
        reverse = d == 1
        sl = pl.ds(pl.multiple_of(gi * GT, GT), GT)
        lw = lw_s[d, sl, :]
        cf = lw
        sh = 1
        while sh < L:
            cf = cf + jnp.where(pos >= sh, pltpu.roll(cf, sh, 0), 0.0)
            sh *= 2
        to3 = lambda x: x.reshape(CG, L, HW)
        cf3, lw3 = to3(cf), to3(lw)
        tot = cf3[:, L - 1:L, :]
        if reverse:
            cin = tot - cf3 + lw3
            cex = tot - cf3
        else:
            cin = cf3
            cex = cf3 - lw3
        a3, r3 = to3(a_s[sl, :].astype(F32)), to3(r_s[sl, :].astype(F32))
        k3, b3 = to3(k_s[d, sl, :].astype(F32)), to3(b_s[d, sl, :].astype(F32))
        v_bf = to3(v_s[sl, :])
        ek = jnp.exp(-cin)
        ekt = jnp.exp(tot - cin)
        return dict(
            at=(a3 * jnp.exp(cex)).astype(BF16), rh=r3 * jnp.exp(cin),
            bh=(b3 * ek).astype(BF16), kh=(k3 * ek).astype(BF16),
            bt=(b3 * ekt).astype(BF16), kt=(k3 * ekt).astype(BF16),
            v=v_bf, dec=jnp.exp(tot))

    def tri(x, fwd_mask, bwd_mask):
        return jnp.concatenate([jnp.where(fwd_mask, x[:CG], 0.0), jnp.where(bwd_mask, x[CG:], 0.0)], axis=0)

    def operands2(gf, gb):
        of, ob = operands(0, gf), operands(1, gb)
        return {name: jnp.concatenate([of[name], ob[name]], axis=0) for name in of}

    def precompute(ops, mt_o, gt_o, rp_o, y1_o, dec_o):
        cat = lambda name: ops[name]
        at, rh, bt, kt, v_bf = cat("at"), cat("rh"), cat("bt"), cat("kt"), cat("v")
        vs = bd(v_bf)
        strict = lambda x: tri(x, trow > scol, trow < scol)
        incl = lambda x: tri(x, trow >= scol, trow <= scol)

        lhs = jnp.concatenate([at, rh.astype(BF16)], axis=1)
        sb = _bmm_nt(lhs, bd(cat("bh")))
        sk = _bmm_nt(lhs, bd(cat("kh")))
        aab = strict(sb[:, :L])
        arb = incl(sb[:, L:]).astype(BF16)
        aak = strict(sk[:, :L]).astype(BF16)
        ark = incl(sk[:, L:]).astype(BF16)

        tinv = eye + aab
        aab_bf = aab.astype(BF16)
        pw = _bmm(aab_bf, bd(aab_bf))
        step = 4
        while step < L:
            both = _bmm(jnp.concatenate([tinv.astype(BF16), pw.astype(BF16)], axis=1), bd(pw.astype(BF16)))
            tinv = tinv + both[:, :L]
            pw = both[:, L:]
            step *= 2
        tinv = (tinv + _bmm(tinv.astype(BF16), bd(pw.astype(BF16)))).astype(BF16)

        akv = _bmm(jnp.concatenate([aak, ark], axis=1), vs)
        ap = _bmm(tinv, bd(at)).astype(BF16)
        u0 = _bmm(tinv, bd(akv[:, :L].astype(BF16))).astype(BF16)
        split = lambda x: x.reshape((2, CG) + x.shape[1:])
        rp_o[...] = split((rh + _bmm(arb, bd(ap))).astype(BF16))
        y1_o[...] = split(_bmm(arb, bd(u0)) + akv[:, L:])
        mt_o[...] = split(_bmm_tn(ap, bt).astype(BF16))
        gt_o[...] = split(_bmm_tn(jnp.concatenate([u0, v_bf], axis=1),
                                  jnp.concatenate([bt, kt], axis=1)))
        dec_o[...] = split(cat("dec"))

    def recur_readout(gf, gb, mt_i, gt_i, rp_i, y1_i, dec_i):
        for i in range(CG):
            for d, j in ((0, i), (1, CG - 1 - i)):
                s = st_s[d]
                s_bf = s.astype(BF16) * bdm
                ss_s[d, j] = s_bf
                st_s[d] = s * dec_i[d, j] + _dot(s_bf, mt_i[d, j]) + gt_i[d, j]
        for d, g in ((0, gf), (1, gb)):
            y = _bmm_nt(rp_i[d], ss_s[d]) + y1_i[d]
            rows = pl.ds(pl.multiple_of((g + 1) * GT, GT), GT)
            y_s[rows, :] += y.reshape(GT, HW)

    buf_a = (mt_a, gt_a, rp_a, y1_a, dec_a)
    buf_b = (mt_b, gt_b, rp_b, y1_b, dec_b)
    for ref in buf_b + (st_s, y_s):
        ref[...] = jnp.zeros_like(ref)

    def pipelined(j, carry):
        g = 2 * j
        precompute(operands2(g, n_groups - 1 - g), *buf_a)
        recur_readout(g - 1, n_groups - g, *buf_b)
        precompute(operands2(g + 1, n_groups - 2 - g), *buf_b)
        recur_readout(g, n_groups - 1 - g, *buf_a)
        return carry

    lax.fori_loop(0, n_groups // 2, pipelined, 0)
    recur_readout(n_groups - 1, 0, *buf_b)

    y = y_s[GT:GT + T, :]
    mean = head_sum(y) * (1.0 / HEAD_DIM)
    yc = y - mean
    var = head_sum(yc * yc) * (1.0 / HEAD_DIM)
    yn = yc * lax.rsqrt(var + LNX_EPS) * lng_ref[...] + lnb_ref[...]
    o_ref[0] = ((yn + bonus) * gate_ref[0].astype(F32)).astype(o_ref.dtype)


def _rwkv(proj3, p):
    B, T, _ = proj3.shape
    HW = RWKV_LANES
    L = RWKV_CHUNK
    n_q = B_WIDTH // HW
    base = (A_WIDTH + 2 * A_KV_WIDTH + A_WIDTH) // HW
    _, cg, n_groups = _rwkv_groups(T)
    vec = lambda a: a.reshape(1, -1)
    blk = lambda off: pl.BlockSpec((1, T, HW), lambda b, h: (b, 0, off + h))
    pvec = lambda off=0: pl.BlockSpec((1, HW), lambda b, h: (0, off + h))
    mu = vec(p["mu"])
    in_specs = [
        blk(base), blk(base + n_q), blk(base + 2 * n_q),
        pl.BlockSpec((1, T, 4 * LORA), lambda b, h: (b, 0, base + 3 * n_q)),
        blk(base + 3 * n_q + 1),
        pvec(0), pvec(n_q), pvec(2 * n_q),
        pl.BlockSpec((1, 4 * LORA), lambda b, h: (0, 3 * n_q)),
        pl.BlockSpec((1, 4 * LORA, 4 * HW), lambda b, h: (h, 0, 0)),
    ] + [pvec()] * 9
    group_bufs = [
        pltpu.VMEM((2, cg, HW, HW), BF16),
        pltpu.VMEM((2, cg, HW, HW), F32),
        pltpu.VMEM((2, cg, L, HW), BF16),
        pltpu.VMEM((2, cg, L, HW), F32),
        pltpu.VMEM((2, cg, 1, HW), F32),
    ]
    scratch = [
        pltpu.VMEM((T, HW), BF16),
        pltpu.VMEM((T, HW), BF16),
        pltpu.VMEM((T, HW), BF16),
        pltpu.VMEM((2, T, HW), BF16),
        pltpu.VMEM((2, T, HW), BF16),
        pltpu.VMEM((2, T, HW), F32),
        pltpu.VMEM(((n_groups + 2) * cg * L, HW), F32),
        pltpu.VMEM((2, HW, HW), F32),
        pltpu.VMEM((2, cg, HW, HW), BF16),
    ] + group_bufs + group_bufs
    return pl.pallas_call(
        _rwkv_kernel,
        grid=(B, n_q),
        in_specs=in_specs,
        out_specs=pl.BlockSpec((1, T, HW), lambda b, h: (b, 0, h)),
        out_shape=jax.ShapeDtypeStruct((B, T, B_WIDTH), BF16),
        scratch_shapes=scratch,
        compiler_params=_cparams(("parallel", "parallel")),
        name="rwkv7",
    )(proj3, proj3, proj3, proj3, proj3, mu, mu, mu, mu, p["wbig"],
      vec(p["w0_f"]), vec(p["a0_f"]), vec(p["w0_b"]), vec(p["a0_b"]),
      vec(p["k_k"]), vec(p["k_a"]), vec(p["r_k"]), vec(p["lnx_g"]), vec(p["lnx_b"]))


def _lora_weights(w2_f, w2_b, a2_f, a2_b):
    HW = RWKV_LANES
    n_q = B_WIDTH // HW
    out = jnp.zeros((n_q, 4 * LORA, 4 * HW), F32)
    for i, w in enumerate((w2_f, w2_b, a2_f, a2_b)):
        wp = w.reshape(LORA, n_q, HW).transpose(1, 0, 2)
        out = out.at[:, i * LORA:(i + 1) * LORA, i * HW:(i + 1) * HW].set(wp)
    return out.astype(BF16)


def _ret_kernel(lg_ref, q_ref, k_ref, v_ref, g_ref, gn_ref, o_ref, st_s, sb_s):
    h = pl.program_id(1)
    T = q_ref.shape[1]
    C = min(RET_CHUNK, T)
    n = T // C
    lgf = lg_ref[h, 0]
    lgb = lg_ref[h, 1]
    idx = lax.broadcasted_iota(jnp.int32, (C, 1), 0).astype(F32)
    ji = lax.broadcasted_iota(jnp.int32, (C, C), 0)
    li = lax.broadcasted_iota(jnp.int32, (C, C), 1)
    diff = (ji - li).astype(F32)
    dmat = jnp.exp(jnp.where(ji >= li, diff * lgf, -diff * lgb))
    dk = q_ref.shape[-1]
    wide = lambda col: jnp.broadcast_to(col, (C, dk)).astype(BF16)
    qdec_f = wide(jnp.exp((idx + 1.0) * lgf))
    kdec_f = wide(jnp.exp((C - 1.0 - idx) * lgf))
    qdec_b = wide(jnp.exp((C - 1.0 - idx) * lgb))
    kdec_b = wide(jnp.exp((idx + 1.0) * lgb))
    cdec_f = jnp.exp(jnp.full((1, 1), float(C), F32) * lgf)
    cdec_b = jnp.exp(jnp.full((1, 1), float(C), F32) * lgb)

    def chunk(i):
        sl = pl.ds(i * C, C)
        return sl, q_ref[0, sl, :], k_ref[0, sl, :], v_ref[0, sl, :]

    st_s[...] = jnp.zeros_like(st_s)
    for c in range(n - 1, -1, -1):
        st = st_s[...]
        sb_s[c] = st.astype(BF16)
        if c > 0:
            _, _, k, v = chunk(c)
            st_s[...] = st * cdec_b + _dot_tn(k * kdec_b, v)

    st_s[...] = jnp.zeros_like(st_s)
    inv_n = 1.0 / o_ref.shape[-1]
    for c in range(n):
        sl, q, k, v = chunk(c)
        st = st_s[...]
        s = (_dot_nt(q, k) * dmat).astype(BF16)
        lhs = jnp.concatenate([s, q * qdec_f, q * qdec_b], axis=1)
        rhs = jnp.concatenate([v, st.astype(BF16), sb_s[c]], axis=0)
        o = _dot(lhs, rhs)
        if c < n - 1:
            st_s[...] = st * cdec_f + _dot_tn(k * kdec_f, v)
        mean = jnp.sum(o, axis=-1, keepdims=True) * inv_n
        oc = o - mean
        var = jnp.sum(oc * oc, axis=-1, keepdims=True) * inv_n
        y = oc * lax.rsqrt(var + GN_EPS) * gn_ref[...]
        o_ref[0, sl, :] = (g_ref[0, sl, :].astype(F32) * y).astype(o_ref.dtype)


def _retention(proj3, gn_g):
    B, T, width = proj3.shape
    dk = (width // 6) // C_HEADS
    dv = 2 * dk
    lg_f = np.log(1.0 - 2.0 ** (-5.0 - np.arange(C_HEADS, dtype=np.float32))).astype(np.float32)
    lg = jnp.asarray(np.stack([lg_f, lg_f[::-1]], axis=1), F32)
    return pl.pallas_call(
        _ret_kernel,
        grid=(B, C_HEADS),
        in_specs=[
            pl.BlockSpec(memory_space=pltpu.SMEM),
            pl.BlockSpec((1, T, dk), lambda b, h: (b, 0, h)),
            pl.BlockSpec((1, T, dk), lambda b, h: (b, 0, C_HEADS + h)),
            pl.BlockSpec((1, T, dv), lambda b, h: (b, 0, C_HEADS + h)),
            pl.BlockSpec((1, T, dv), lambda b, h: (b, 0, 2 * C_HEADS + h)),
            pl.BlockSpec((1, dv), lambda b, h: (0, h)),
        ],
        out_specs=pl.BlockSpec((1, T, dv), lambda b, h: (b, 0, h)),
        out_shape=jax.ShapeDtypeStruct((B, T, C_HEADS * dv), BF16),
        scratch_shapes=[pltpu.VMEM((dk, dv), F32), pltpu.VMEM((T // min(RET_CHUNK, T), dk, dv), BF16)],
        compiler_params=_cparams(("parallel", "parallel")),
        name="retention",
    )(lg, proj3, proj3, proj3, proj3, gn_g.reshape(1, -1))


def kernel(x, pre_gain, post_gain, even_w_in, even_mu, even_q_gain, even_k_gain, even_k_k, even_k_a, even_r_k, even_w0_f, even_w2_f, even_a0_f, even_a2_f, even_w0_b, even_w2_b, even_a0_b, even_a2_b, even_lnx_g, even_lnx_b, even_w_out, odd_w_in, odd_gn_g, odd_w_out):
    B, T, D = x.shape
    n_tok = B * T
    depth = pre_gain.shape[0]
    tm = min(512, T)
    tq = min(1024, T)
    cos_a, sin_a = _rope_tables(T, HEAD_DIM)
    cos_a, sin_a = jnp.tile(cos_a, (1, 2)), jnp.tile(sin_a, (1, 2))
    dk = odd_w_in.shape[-1] // 6 // C_HEADS
    cos_c, sin_c = _rope_tables(T, dk)

    h = x.reshape(n_tok, D)
    for layer in range(depth):
        i = layer // 2
        if layer % 2 == 0:
            proj = _proj(h, pre_gain[layer], even_w_in[i].astype(BF16), cos_a, sin_a, tm=tm, col_chunk=MXU_DIM,
                         seq=T, head_gains=(even_q_gain[i], even_k_gain[i]))
            proj3 = proj.reshape(B, T, -1)
            out_a = _attention(proj3, tq=tq)
            params = dict(
                mu=even_mu[i], k_k=even_k_k[i], k_a=even_k_a[i], r_k=even_r_k[i],
                w0_f=even_w0_f[i], a0_f=even_a0_f[i], w0_b=even_w0_b[i], a0_b=even_a0_b[i],
                lnx_g=even_lnx_g[i], lnx_b=even_lnx_b[i],
                wbig=_lora_weights(even_w2_f[i], even_w2_b[i], even_a2_f[i], even_a2_b[i]),
            )
            out_b = _rwkv(proj3, params)
            mixed = [out_a.reshape(n_tok, -1), out_b.reshape(n_tok, -1)]
            h = _out_proj(mixed, even_w_out[i].astype(BF16), post_gain[layer], h, tm=tm)
        else:
            proj = _proj(h, pre_gain[layer], odd_w_in[i].astype(BF16), cos_c, sin_c, tm=tm, col_chunk=1024,
                         seq=T, k_scale=dk ** -0.5)
            ret = _retention(proj.reshape(B, T, -1), odd_gn_g[i])
            h = _out_proj([ret.reshape(n_tok, -1)], odd_w_out[i].astype(BF16), post_gain[layer], h, tm=tm)
    return h.reshape(B, T, D)
```

```python
import functools

import numpy as np
import jax
import jax.numpy as jnp
from jax import lax
from jax.experimental import pallas as pl
from jax.experimental.pallas import tpu as pltpu

F32 = jnp.float32
BF16 = jnp.bfloat16

GRID_W = 64
ROPE_THETA = 10000.0
NORM_EPS = 1e-6

A_HEADS = 8
A_KV_HEADS = 2
HEAD_DIM = 64
A_WIDTH = A_HEADS * HEAD_DIM
A_KV_WIDTH = A_KV_HEADS * HEAD_DIM
B_HEADS = 8
B_WIDTH = B_HEADS * HEAD_DIM
LORA = 64
LNX_EPS = 64e-5
C_HEADS = 4
GN_EPS = 1e-5

LANES = 128
MXU_DIM = 256
RWKV_CHUNK = 64
RWKV_GROUP = 4
RET_CHUNK = 256
VMEM_LIMIT = 56 * 1024 * 1024


def _cparams(sem):
    return pltpu.CompilerParams(dimension_semantics=sem, vmem_limit_bytes=VMEM_LIMIT)


def _dot(a, b):
    return jnp.dot(a, b, preferred_element_type=F32)


def _dot_nt(a, b):
    return lax.dot_general(a, b, (((1,), (1,)), ((), ())), preferred_element_type=F32)


def _dot_tn(a, b):
    return lax.dot_general(a, b, (((0,), (0,)), ((), ())), preferred_element_type=F32)


def _bmm(a, b):
    return lax.dot_general(a, b, (((2,), (1,)), ((0,), (0,))), preferred_element_type=F32)


def _bmm_nt(a, b):
    return lax.dot_general(a, b, (((2,), (2,)), ((0,), (0,))), preferred_element_type=F32)


def _bmm_tn(a, b):
    return lax.dot_general(a, b, (((1,), (1,)), ((0,), (0,))), preferred_element_type=F32)


def _sigmoid(x):
    return 1.0 / (1.0 + jnp.exp(-x))


def _silu(x):
    return x * _sigmoid(x)


def _rope_tables(T, dim):
    half = dim // 2
    t = np.arange(T)
    row = (t // GRID_W).astype(np.float32)
    col = (t % GRID_W).astype(np.float32)
    inv_freq = (ROPE_THETA ** (-np.arange(0, half, 2, dtype=np.float32) / half)).astype(np.float32)
    ar = row[:, None] * inv_freq
    ac = col[:, None] * inv_freq
    cos = np.concatenate([np.cos(ar), np.cos(ar), np.cos(ac), np.cos(ac)], axis=1)
    sin = np.concatenate([-np.sin(ar), np.sin(ar), -np.sin(ac), np.sin(ac)], axis=1)
    return jnp.asarray(cos, F32), jnp.asarray(sin, F32)


def _head_norm_rope(x, gain, cos, sin):
    lane = lax.broadcasted_iota(jnp.int32, (1, LANES), 1)
    head0 = lane < HEAD_DIM
    sq = x * x
    s0 = jnp.sum(jnp.where(head0, sq, 0.0), axis=-1, keepdims=True)
    s1 = jnp.sum(jnp.where(head0, 0.0, sq), axis=-1, keepdims=True)
    ms = jnp.where(head0, s0, s1) * (1.0 / HEAD_DIM)
    y = x * lax.rsqrt(ms + NORM_EPS) * gain
    q4 = HEAD_DIM // 4
    first = (lane % (2 * q4)) < q4
    partner = jnp.where(first, pltpu.roll(y, LANES - q4, 1), pltpu.roll(y, q4, 1))
    return y * cos + partner * sin


def _even_col_kind(c):
    a_gate = A_WIDTH + 2 * A_KV_WIDTH
    b_gate = a_gate + A_WIDTH + 3 * B_WIDTH + 4 * LORA
    if c < A_WIDTH:
        return "q"
    if c < A_WIDTH + A_KV_WIDTH:
        return "k"
    if a_gate <= c < a_gate + A_WIDTH or c >= b_gate:
        return "silu"
    return "plain"


def _odd_col_kind(c, n_out):
    if c < n_out // 6:
        return "rope_q"
    if c < n_out // 3:
        return "rope_k"
    if c >= 2 * n_out // 3:
        return "silu"
    return "plain"


def _proj_kernel(x_ref, *refs, **static):
    _proj_body(x_ref[...], *refs, **static)


def _proj_body(x, g_ref, w_ref, cos_ref, sin_ref, *rest, col_chunk, even, k_scale):
    if even:
        qg_ref, kg_ref, o_ref = rest
    else:
        (o_ref,) = rest
    ms = jnp.mean(x * x, axis=-1, keepdims=True)
    hn = (x * lax.rsqrt(ms + NORM_EPS) * g_ref[...]).astype(BF16)
    n_out = o_ref.shape[-1]
    q_scale = HEAD_DIM ** -0.5 * np.log2(np.e)
    for c0 in range(0, n_out, col_chunk):
        acc = _dot(hn, w_ref[:, c0:c0 + col_chunk])
        for b0 in range(0, col_chunk, LANES):
            c = c0 + b0
            xb = acc[:, b0:b0 + LANES]
            kind = _even_col_kind(c) if even else _odd_col_kind(c, n_out)
            if kind == "q":
                yb = _head_norm_rope(xb, qg_ref[...], cos_ref[...], sin_ref[...]) * q_scale
            elif kind == "k":
                yb = _head_norm_rope(xb, kg_ref[...], cos_ref[...], sin_ref[...])
            elif kind in ("rope_q", "rope_k"):
                t0 = c % (2 * LANES)
                yb = xb * cos_ref[:, t0:t0 + LANES] + pltpu.roll(xb, LANES // 2, 1) * sin_ref[:, t0:t0 + LANES]
                if kind == "rope_k":
                    yb = yb * k_scale
            elif kind == "silu":
                yb = _silu(xb)
            else:
                yb = xb
            o_ref[:, c:c + LANES] = yb.astype(o_ref.dtype)


def _resident(shape):
    return pl.BlockSpec(shape, lambda i: (0,) * len(shape), pipeline_mode=pl.Buffered(1))


def _proj_operands(gain, w_bf, cos, sin, *, tm, seq, head_gains):
    d, n_out = w_bf.shape
    tiles_per_seq = seq // tm
    table = pl.BlockSpec((tm, cos.shape[1]), lambda i: (i % tiles_per_seq, 0))
    in_specs = [_resident((1, d)), _resident((d, n_out)), table, table]
    args = [gain.reshape(1, d), w_bf, cos, sin]
    if head_gains is not None:
        in_specs += [_resident((1, LANES))] * 2
        args += [jnp.tile(g, LANES // HEAD_DIM).reshape(1, LANES) for g in head_gains]
    return in_specs, args


def _proj(h2, operands, static, *, tm):
    n_tok, d = h2.shape
    in_specs, args = operands
    n_out = args[1].shape[1]
    return pl.pallas_call(
        functools.partial(_proj_kernel, **static),
        grid=(n_tok // tm,),
        in_specs=[pl.BlockSpec((tm, d), lambda i: (i, 0))] + in_specs,
        out_specs=pl.BlockSpec((tm, n_out), lambda i: (i, 0)),
        out_shape=jax.ShapeDtypeStruct((n_tok, n_out), BF16),
        compiler_params=_cparams(("parallel",)),
        name="proj",
    )(h2, *args)


def _out_kernel(*refs, n_in, nxt):
    ins = refs[:n_in]
    w_ref, g_ref, h_ref = refs[n_in:n_in + 3]
    acc = None
    r0 = 0
    for m_ref in ins:
        k = m_ref.shape[-1]
        part = _dot(m_ref[...], w_ref[r0:r0 + k, :])
        acc = part if acc is None else acc + part
        r0 += k
    ms = jnp.mean(acc * acc, axis=-1, keepdims=True)
    h_new = h_ref[...] + acc * lax.rsqrt(ms + NORM_EPS) * g_ref[...]
    if nxt is None:
        (o_ref,) = refs[n_in + 3:]
    else:
        *proj_refs, o_ref, p_ref = refs[n_in + 3:]
        _proj_body(h_new, *proj_refs, p_ref, **nxt)
    o_ref[...] = h_new


def _out_proj(mixed, w_bf, gain, h2, *, tm, nxt=None):
    n_tok, d = h2.shape
    k_total = w_bf.shape[0]
    in_specs = [pl.BlockSpec((tm, m.shape[1]), lambda i: (i, 0)) for m in mixed]
    in_specs += [_resident((k_total, d)), _resident((1, d)), pl.BlockSpec((tm, d), lambda i: (i, 0))]
    args = [*mixed, w_bf, gain.reshape(1, d), h2]
    out_specs = pl.BlockSpec((tm, d), lambda i: (i, 0))
    out_shape = jax.ShapeDtypeStruct((n_tok, d), F32)
    static = None
    if nxt is not None:
        (nxt_specs, nxt_args), static = nxt
        n_next = nxt_args[1].shape[1]
        in_specs += nxt_specs
        args += nxt_args
        out_specs = [out_specs, pl.BlockSpec((tm, n_next), lambda i: (i, 0))]
        out_shape = [out_shape, jax.ShapeDtypeStruct((n_tok, n_next), BF16)]
    return pl.pallas_call(
        functools.partial(_out_kernel, n_in=len(mixed), nxt=static),
        grid=(n_tok // tm,),
        in_specs=in_specs,
        out_specs=out_specs,
        out_shape=out_shape,
        compiler_params=_cparams(("parallel",)),
        name="out_proj",
    )(*args)


ONES_ROWS = 16


def _attn_kernel(q_ref, kv_ref, ga_ref, gb_ref, o_ref, vt_ref, *, tq):
    t_idx = pl.program_id(1)
    T = kv_ref.shape[1]
    lane = lax.broadcasted_iota(jnp.int32, (1, LANES), 1)
    head0 = lane < HEAD_DIM

    @pl.when(t_idx == 0)
    def _():
        vt = kv_ref[0, :, LANES:2 * LANES].astype(F32).T
        for g in range(A_KV_HEADS):
            vt_ref[g, 0:HEAD_DIM, :] = vt[g * HEAD_DIM:(g + 1) * HEAD_DIM].astype(BF16)
            vt_ref[g, HEAD_DIM:HEAD_DIM + ONES_ROWS, :] = jnp.ones((ONES_ROWS, T), BF16)

    row0 = pl.multiple_of(t_idx * tq, tq)
    kn = kv_ref[0, :, 0:LANES]
    rep = A_HEADS // A_KV_HEADS

    def scores(head):
        pair, sub, g = head // 2, head % 2, head // rep
        qn = q_ref[0, pl.ds(row0, tq), pair * LANES:(pair + 1) * LANES].astype(F32)
        src = qn if sub == g else pltpu.roll(qn, HEAD_DIM, 1)
        grp = head0 if g == 0 else jnp.logical_not(head0)
        return _dot_nt(kn, jnp.where(grp, src, 0.0).astype(BF16))

    def values(head, st):
        p = jnp.exp2(st - jnp.max(st, axis=0, keepdims=True)).astype(BF16)
        ox = _dot(vt_ref[head // rep], p)
        return ox[0:HEAD_DIM] * (1.0 / ox[HEAD_DIM:HEAD_DIM + 1])

    outs = []
    st = scores(0)
    for head in range(A_HEADS):
        st_next = scores(head + 1) if head + 1 < A_HEADS else None
        outs.append(values(head, st))
        st = st_next
        if head % 2 == 1:
            pair = head // 2
            o_pair = jnp.concatenate(outs, axis=0).T
            outs = []
            gref = ga_ref if pair < 2 else gb_ref
            gc = (pair % 2) * LANES
            gate = gref[0, pl.ds(row0, tq), gc:gc + LANES].astype(F32)
            o_ref[0, :, pair * LANES:(pair + 1) * LANES] = (o_pair * gate).astype(o_ref.dtype)


def _attention(proj3, *, tq):
    B, T, _ = proj3.shape
    return pl.pallas_call(
        functools.partial(_attn_kernel, tq=tq),
        grid=(B, T // tq),
        in_specs=[
            pl.BlockSpec((1, T, A_WIDTH), lambda b, t: (b, 0, 0)),
            pl.BlockSpec((1, T, 2 * A_KV_WIDTH), lambda b, t: (b, 0, A_WIDTH // (2 * A_KV_WIDTH))),
            pl.BlockSpec((1, T, 2 * LANES), lambda b, t: (b, 0, 3)),
            pl.BlockSpec((1, T, 2 * LANES), lambda b, t: (b, 0, 4)),
        ],
        out_specs=pl.BlockSpec((1, tq, A_WIDTH), lambda b, t: (b, t, 0)),
        out_shape=jax.ShapeDtypeStruct((B, T, A_WIDTH), BF16),
        scratch_shapes=[pltpu.VMEM((A_KV_HEADS, HEAD_DIM + ONES_ROWS, T), BF16)],
        compiler_params=_cparams(("parallel", "arbitrary")),
        name="attention",
    )(proj3, proj3, proj3, proj3)


RWKV_LANES = MXU_DIM


def _rwkv_groups(T):
    n_chunks = T // RWKV_CHUNK
    cg = min(RWKV_GROUP, n_chunks // 2)
    n_groups = n_chunks // cg
    assert n_groups % 2 == 0 and n_groups * cg == n_chunks
    return n_chunks, cg, n_groups


def _rwkv_kernel(r_ref, k_ref, v_ref, lo_ref, gate_ref,
                 mur_ref, muk_ref, muv_ref, mulo_ref, wbig_ref,
                 w0f_ref, a0f_ref, w0b_ref, a0b_ref,
                 kk_ref, ka_ref, rk_ref, lng_ref, lnb_ref,
                 o_ref,
                 r_s, v_s, a_s, k_s, b_s, lw_s, y_s, st_s, ss_s,
                 mt_a, gt_a, rp_a, y1_a, dec_a, mt_b, gt_b, rp_b, y1_b, dec_b):
    T = r_ref.shape[1]
    L = RWKV_CHUNK
    HW = RWKV_LANES
    assert L == HEAD_DIM
    n_chunks, CG, n_groups = _rwkv_groups(T)
    GT = CG * L
    lane = lax.broadcasted_iota(jnp.int32, (1, LANES), 1)
    head0 = lane < HEAD_DIM

    SHIFT_ROWS = LANES
    n_shift = T // SHIFT_ROWS
    win_start = [min(max(i * SHIFT_ROWS - (MXU_DIM - SHIFT_ROWS) // 2, 0), T - MXU_DIM) for i in range(n_shift)]
    t_in = lax.broadcasted_iota(jnp.int32, (SHIFT_ROWS, MXU_DIM), 0)
    j_in = lax.broadcasted_iota(jnp.int32, (SHIFT_ROWS, MXU_DIM), 1)
    pick = {off: jnp.where((j_in == t_in + off - 1) | (j_in == t_in + off + 1), 1.0, 0.0).astype(BF16)
            for off in sorted({i * SHIFT_ROWS - s for i, s in enumerate(win_start)})}

    def shift(ref, mu_ref):
        u = ref[0].astype(F32)
        both = jnp.concatenate(
            [_dot(pick[i * SHIFT_ROWS - s], ref[0, s:s + MXU_DIM, :]) for i, s in enumerate(win_start)], axis=0)
        return u + mu_ref[...] * (0.5 * both - u)

    def head_sum(x):
        parts = []
        for c0 in range(0, HW, LANES):
            xc = x[:, c0:c0 + LANES]
            s0 = jnp.sum(jnp.where(head0, xc, 0.0), axis=-1, keepdims=True)
            s1 = jnp.sum(jnp.where(head0, 0.0, xc), axis=-1, keepdims=True)
            parts.append(jnp.where(head0, s0, s1))
        return jnp.concatenate(parts, axis=1)

    r = shift(r_ref, mur_ref)
    kb = shift(k_ref, muk_ref)
    vb = shift(v_ref, muv_ref)
    lo = shift(lo_ref, mulo_ref)
    lo_t = jnp.concatenate([jnp.tanh(lo[:, :2 * LORA]), lo[:, 2 * LORA:]], axis=1).astype(BF16)
    z = _dot(lo_t, wbig_ref[0])

    kkv = kb * kk_ref[...]
    kkn = kkv * lax.rsqrt(jnp.maximum(head_sum(kkv * kkv), 1e-24))
    bonus = head_sum(r * kb * rk_ref[...]) * vb

    r_s[...] = r.astype(BF16)
    v_s[...] = vb.astype(BF16)
    a_s[...] = (-kkn).astype(BF16)
    for d, (w0_ref, a0_ref) in enumerate(((w0f_ref, a0f_ref), (w0b_ref, a0b_ref))):
        xw = w0_ref[...] + z[:, d * HW:(d + 1) * HW]
        lw_s[d] = (-np.exp(-0.5)) * _sigmoid(xw)
        iclr = _sigmoid(a0_ref[...] + z[:, (2 + d) * HW:(3 + d) * HW])
        k_s[d] = (kb * (1.0 + (iclr - 1.0) * ka_ref[...])).astype(BF16)
        b_s[d] = (kkn * iclr).astype(BF16)

    pos = lax.broadcasted_iota(jnp.int32, (GT, 1), 0) % L
    trow = lax.broadcasted_iota(jnp.int32, (L, HW), 0)
    scol = lax.broadcasted_iota(jnp.int32, (L, HW), 1) % L
    eye = (trow == scol).astype(F32)
    blk_r = lax.broadcasted_iota(jnp.int32, (HW, HW), 0) // L
    blk_c = lax.broadcasted_iota(jnp.int32, (HW, HW), 1) // HEAD_DIM
    bdm = jnp.where(blk_r == blk_c, 1.0, 0.0).astype(BF16)

    def bd(x):
        return jnp.concatenate([x] * (HW // L), axis=1) * bdm

    def operands(d, gi):
        reverse = d == 1
        sl = pl.ds(pl.multiple_of(gi * GT, GT), GT)
        lw = lw_s[d, sl, :]
        cf = lw
        sh = 1
        while sh < L:
            cf = cf + jnp.where(pos >= sh, pltpu.roll(cf, sh, 0), 0.0)
            sh *= 2
        to3 = lambda x: x.reshape(CG, L, HW)
        cf3, lw3 = to3(cf), to3(lw)
        tot = cf3[:, L - 1:L, :]
        if reverse:
            cin = tot - cf3 + lw3
            cex = tot - cf3
        else:
            cin = cf3
            cex = cf3 - lw3
        a3, r3 = to3(a_s[sl, :].astype(F32)), to3(r_s[sl, :].astype(F32))
        k3, b3 = to3(k_s[d, sl, :].astype(F32)), to3(b_s[d, sl, :].astype(F32))
        v_bf = to3(v_s[sl, :])
        ek = jnp.exp(-cin)
        ekt = jnp.exp(tot - cin)
        return dict(
            at=(a3 * jnp.exp(cex)).astype(BF16), rh=r3 * jnp.exp(cin),
            bh=(b3 * ek).astype(BF16), kh=(k3 * ek).astype(BF16),
            bt=(b3 * ekt).astype(BF16), kt=(k3 * ekt).astype(BF16),
            v=v_bf, dec=jnp.exp(tot))

    def tri(x, fwd_mask, bwd_mask):
        return jnp.concatenate([jnp.where(fwd_mask, x[:CG], 0.0), jnp.where(bwd_mask, x[CG:], 0.0)], axis=0)

    def operands2(gf, gb):
        of, ob = operands(0, gf), operands(1, gb)
        return {name: jnp.concatenate([of[name], ob[name]], axis=0) for name in of}

    def precompute(ops, mt_o, gt_o, rp_o, y1_o, dec_o):
        cat = lambda name: ops[name]
        at, rh, bt, kt, v_bf = cat("at"), cat("rh"), cat("bt"), cat("kt"), cat("v")
        vs = bd(v_bf)
        strict = lambda x: tri(x, trow > scol, trow < scol)
        incl = lambda x: tri(x, trow >= scol, trow <= scol)

        lhs = jnp.concatenate([at, rh.astype(BF16)], axis=1)
        sb = _bmm_nt(lhs, bd(cat("bh")))
        sk = _bmm_nt(lhs, bd(cat("kh")))
        aab = strict(sb[:, :L])
        arb = incl(sb[:, L:]).astype(BF16)
        aak = strict(sk[:, :L]).astype(BF16)
        ark = incl(sk[:, L:]).astype(BF16)

        tinv = eye + aab
        aab_bf = aab.astype(BF16)
        pw = _bmm(aab_bf, bd(aab_bf))
        step = 4
        while step < L:
            both = _bmm(jnp.concatenate([tinv.astype(BF16), pw.astype(BF16)], axis=1), bd(pw.astype(BF16)))
            tinv = tinv + both[:, :L]
            pw = both[:, L:]
            step *= 2
        tinv = (tinv + _bmm(tinv.astype(BF16), bd(pw.astype(BF16)))).astype(BF16)

        akv = _bmm(jnp.concatenate([aak, ark], axis=1), vs)
        ap = _bmm(tinv, bd(at)).astype(BF16)
        u0 = _bmm(tinv, bd(akv[:, :L].astype(BF16))).astype(BF16)
        split = lambda x: x.reshape((2, CG) + x.shape[1:])
        rp_o[...] = split((rh + _bmm(arb, bd(ap))).astype(BF16))
        y1_o[...] = split(_bmm(arb, bd(u0)) + akv[:, L:])
        mt_o[...] = split(_bmm_tn(ap, bt).astype(BF16))
        gt_o[...] = split(_bmm_tn(jnp.concatenate([u0, v_bf], axis=1),
                                  jnp.concatenate([bt, kt], axis=1)))
        dec_o[...] = split(cat("dec"))

    def recur_readout(gf, gb, mt_i, gt_i, rp_i, y1_i, dec_i):
        for i in range(CG):
            for d, j in ((0, i), (1, CG - 1 - i)):
                s = st_s[d]
                s_bf = s.astype(BF16) * bdm
                ss_s[d, j] = s_bf
                st_s[d] = s * dec_i[d, j] + _dot(s_bf, mt_i[d, j]) + gt_i[d, j]
        for d, g in ((0, gf), (1, gb)):
            y = _bmm_nt(rp_i[d], ss_s[d]) + y1_i[d]
            rows = pl.ds(pl.multiple_of((g + 1) * GT, GT), GT)
            y_s[rows, :] += y.reshape(GT, HW)

    buf_a = (mt_a, gt_a, rp_a, y1_a, dec_a)
    buf_b = (mt_b, gt_b, rp_b, y1_b, dec_b)
    for ref in buf_b + (st_s, y_s):
        ref[...] = jnp.zeros_like(ref)

    def pipelined(j, carry):
        g = 2 * j
        precompute(operands2(g, n_groups - 1 - g), *buf_a)
        recur_readout(g - 1, n_groups - g, *buf_b)
        precompute(operands2(g + 1, n_groups - 2 - g), *buf_b)
        recur_readout(g, n_groups - 1 - g, *buf_a)
        return carry

    lax.fori_loop(0, n_groups // 2, pipelined, 0)
    recur_readout(n_groups - 1, 0, *buf_b)

    y = y_s[GT:GT + T, :]
    mean = head_sum(y) * (1.0 / HEAD_DIM)
    yc = y - mean
    var = head_sum(yc * yc) * (1.0 / HEAD_DIM)
    yn = yc * lax.rsqrt(var + LNX_EPS) * lng_ref[...] + lnb_ref[...]
    o_ref[0] = ((yn + bonus) * gate_ref[0].astype(F32)).astype(o_ref.dtype)


def _rwkv(proj3, p):
    B, T, _ = proj3.shape
    HW = RWKV_LANES
    L = RWKV_CHUNK
    n_q = B_WIDTH // HW
    base = (A_WIDTH + 2 * A_KV_WIDTH + A_WIDTH) // HW
    _, cg, n_groups = _rwkv_groups(T)
    vec = lambda a: a.reshape(1, -1)
    blk = lambda off: pl.BlockSpec((1, T, HW), lambda b, h: (b, 0, off + h))
    pvec = lambda off=0: pl.BlockSpec((1, HW), lambda b, h: (0, off + h))
    mu = vec(p["mu"])
    in_specs = [
        blk(base), blk(base + n_q), blk(base + 2 * n_q),
        pl.BlockSpec((1, T, 4 * LORA), lambda b, h: (b, 0, base + 3 * n_q)),
        blk(base + 3 * n_q + 1),
        pvec(0), pvec(n_q), pvec(2 * n_q),
        pl.BlockSpec((1, 4 * LORA), lambda b, h: (0, 3 * n_q)),
        pl.BlockSpec((1, 4 * LORA, 4 * HW), lambda b, h: (h, 0, 0)),
    ] + [pvec()] * 9
    group_bufs = [
        pltpu.VMEM((2, cg, HW, HW), BF16),
        pltpu.VMEM((2, cg, HW, HW), F32),
        pltpu.VMEM((2, cg, L, HW), BF16),
        pltpu.VMEM((2, cg, L, HW), F32),
        pltpu.VMEM((2, cg, 1, HW), F32),
    ]
    scratch = [
        pltpu.VMEM((T, HW), BF16),
        pltpu.VMEM((T, HW), BF16),
        pltpu.VMEM((T, HW), BF16),
        pltpu.VMEM((2, T, HW), BF16),
        pltpu.VMEM((2, T, HW), BF16),
        pltpu.VMEM((2, T, HW), F32),
        pltpu.VMEM(((n_groups + 2) * cg * L, HW), F32),
        pltpu.VMEM((2, HW, HW), F32),
        pltpu.VMEM((2, cg, HW, HW), BF16),
    ] + group_bufs + group_bufs
    return pl.pallas_call(
        _rwkv_kernel,
        grid=(B, n_q),
        in_specs=in_specs,
        out_specs=pl.BlockSpec((1, T, HW), lambda b, h: (b, 0, h)),
        out_shape=jax.ShapeDtypeStruct((B, T, B_WIDTH), BF16),
        scratch_shapes=scratch,
        compiler_params=_cparams(("parallel", "parallel")),
        name="rwkv7",
    )(proj3, proj3, proj3, proj3, proj3, mu, mu, mu, mu, p["wbig"],
      vec(p["w0_f"]), vec(p["a0_f"]), vec(p["w0_b"]), vec(p["a0_b"]),
      vec(p["k_k"]), vec(p["k_a"]), vec(p["r_k"]), vec(p["lnx_g"]), vec(p["lnx_b"]))


def _lora_weights(w2_f, w2_b, a2_f, a2_b):
    HW = RWKV_LANES
    n_q = B_WIDTH // HW
    out = jnp.zeros((n_q, 4 * LORA, 4 * HW), F32)
    for i, w in enumerate((w2_f, w2_b, a2_f, a2_b)):
        wp = w.reshape(LORA, n_q, HW).transpose(1, 0, 2)
        out = out.at[:, i * LORA:(i + 1) * LORA, i * HW:(i + 1) * HW].set(wp)
    return out.astype(BF16)


def _ret_kernel(lg_ref, q_ref, k_ref, v_ref, g_ref, gn_ref, o_ref, st_s, sb_s):
    h = pl.program_id(1)
    T = q_ref.shape[1]
    C = min(RET_CHUNK, T)
    n = T // C
    lgf = lg_ref[h, 0]
    lgb = lg_ref[h, 1]
    idx = lax.broadcasted_iota(jnp.int32, (C, 1), 0).astype(F32)
    ji = lax.broadcasted_iota(jnp.int32, (C, C), 0)
    li = lax.broadcasted_iota(jnp.int32, (C, C), 1)
    diff = (ji - li).astype(F32)
    dmat = jnp.exp(jnp.where(ji >= li, diff * lgf, -diff * lgb))
    dk = q_ref.shape[-1]
    wide = lambda col: jnp.broadcast_to(col, (C, dk)).astype(BF16)
    qdec_f = wide(jnp.exp((idx + 1.0) * lgf))
    kdec_f = wide(jnp.exp((C - 1.0 - idx) * lgf))
    qdec_b = wide(jnp.exp((C - 1.0 - idx) * lgb))
    kdec_b = wide(jnp.exp((idx + 1.0) * lgb))
    cdec_f = jnp.exp(jnp.full((1, 1), float(C), F32) * lgf)
    cdec_b = jnp.exp(jnp.full((1, 1), float(C), F32) * lgb)

    def chunk(i):
        sl = pl.ds(i * C, C)
        return sl, q_ref[0, sl, :], k_ref[0, sl, :], v_ref[0, sl, :]

    st_s[...] = jnp.zeros_like(st_s)
    for c in range(n - 1, -1, -1):
        st = st_s[...]
        sb_s[c] = st.astype(BF16)
        if c > 0:
            _, _, k, v = chunk(c)
            st_s[...] = st * cdec_b + _dot_tn(k * kdec_b, v)

    st_s[...] = jnp.zeros_like(st_s)
    inv_n = 1.0 / o_ref.shape[-1]
    for c in range(n):
        sl, q, k, v = chunk(c)
        st = st_s[...]
        s = (_dot_nt(q, k) * dmat).astype(BF16)
        lhs = jnp.concatenate([s, q * qdec_f, q * qdec_b], axis=1)
        rhs = jnp.concatenate([v, st.astype(BF16), sb_s[c]], axis=0)
        o = _dot(lhs, rhs)
        if c < n - 1:
            st_s[...] = st * cdec_f + _dot_tn(k * kdec_f, v)
        mean = jnp.sum(o, axis=-1, keepdims=True) * inv_n
        oc = o - mean
        var = jnp.sum(oc * oc, axis=-1, keepdims=True) * inv_n
        y = oc * lax.rsqrt(var + GN_EPS) * gn_ref[...]
        o_ref[0, sl, :] = (g_ref[0, sl, :].astype(F32) * y).astype(o_ref.dtype)


def _retention(proj3, gn_g):
    B, T, width = proj3.shape
    dk = (width // 6) // C_HEADS
    dv = 2 * dk
    lg_f = np.log(1.0 - 2.0 ** (-5.0 - np.arange(C_HEADS, dtype=np.float32))).astype(np.float32)
    lg = jnp.asarray(np.stack([lg_f, lg_f[::-1]], axis=1), F32)
    return pl.pallas_call(
        _ret_kernel,
        grid=(B, C_HEADS),
        in_specs=[
            pl.BlockSpec(memory_space=pltpu.SMEM),
            pl.BlockSpec((1, T, dk), lambda b, h: (b, 0, h)),
            pl.BlockSpec((1, T, dk), lambda b, h: (b, 0, C_HEADS + h)),
            pl.BlockSpec((1, T, dv), lambda b, h: (b, 0, C_HEADS + h)),
            pl.BlockSpec((1, T, dv), lambda b, h: (b, 0, 2 * C_HEADS + h)),
            pl.BlockSpec((1, dv), lambda b, h: (0, h)),
        ],
        out_specs=pl.BlockSpec((1, T, dv), lambda b, h: (b, 0, h)),
        out_shape=jax.ShapeDtypeStruct((B, T, C_HEADS * dv), BF16),
        scratch_shapes=[pltpu.VMEM((dk, dv), F32), pltpu.VMEM((T // min(RET_CHUNK, T), dk, dv), BF16)],
        compiler_params=_cparams(("parallel", "parallel")),
        name="retention",
    )(lg, proj3, proj3, proj3, proj3, gn_g.reshape(1, -1))


def kernel(x, pre_gain, post_gain, even_w_in, even_mu, even_q_gain, even_k_gain, even_k_k, even_k_a, even_r_k, even_w0_f, even_w2_f, even_a0_f, even_a2_f, even_w0_b, even_w2_b, even_a0_b, even_a2_b, even_lnx_g, even_lnx_b, even_w_out, odd_w_in, odd_gn_g, odd_w_out):
    B, T, D = x.shape
    n_tok = B * T
    depth = pre_gain.shape[0]
    tm = min(512, T)
    tq = min(1024, T)
    cos_a, sin_a = _rope_tables(T, HEAD_DIM)
    cos_a, sin_a = jnp.tile(cos_a, (1, 2)), jnp.tile(sin_a, (1, 2))
    dk = odd_w_in.shape[-1] // 6 // C_HEADS
    cos_c, sin_c = _rope_tables(T, dk)

    def projection(layer):
        i = layer // 2
        if layer % 2 == 0:
            operands = _proj_operands(pre_gain[layer], even_w_in[i].astype(BF16), cos_a, sin_a, tm=tm, seq=T,
                                      head_gains=(even_q_gain[i], even_k_gain[i]))
            return operands, dict(col_chunk=MXU_DIM, even=True, k_scale=1.0)
        operands = _proj_operands(pre_gain[layer], odd_w_in[i].astype(BF16), cos_c, sin_c, tm=tm, seq=T,
                                  head_gains=None)
        return operands, dict(col_chunk=1024, even=False, k_scale=dk ** -0.5)

    def finish_layer(layer, mixed, w_out, h):
        w_bf = w_out.astype(BF16)
        if layer + 1 == depth:
            return _out_proj(mixed, w_bf, post_gain[layer], h, tm=tm), None
        return _out_proj(mixed, w_bf, post_gain[layer], h, tm=tm, nxt=projection(layer + 1))

    h = x.reshape(n_tok, D)
    proj = _proj(h, *projection(0), tm=tm)
    for layer in range(depth):
        i = layer // 2
        if layer % 2 == 0:
            proj3 = proj.reshape(B, T, -1)
            out_a = _attention(proj3, tq=tq)
            params = dict(
                mu=even_mu[i], k_k=even_k_k[i], k_a=even_k_a[i], r_k=even_r_k[i],
                w0_f=even_w0_f[i], a0_f=even_a0_f[i], w0_b=even_w0_b[i], a0_b=even_a0_b[i],
                lnx_g=even_lnx_g[i], lnx_b=even_lnx_b[i],
                wbig=_lora_weights(even_w2_f[i], even_w2_b[i], even_a2_f[i], even_a2_b[i]),
            )
            out_b = _rwkv(proj3, params)
            mixed = [out_a.reshape(n_tok, -1), out_b.reshape(n_tok, -1)]
            h, proj = finish_layer(layer, mixed, even_w_out[i], h)
        else:
            ret = _retention(proj.reshape(B, T, -1), odd_gn_g[i])
            h, proj = finish_layer(layer, [ret.reshape(n_tok, -1)], odd_w_out[i], h)
    return h.reshape(B, T, D)
```

```python
import functools

import numpy as np
import jax
import jax.numpy as jnp
from jax import lax
from jax.experimental import pallas as pl
from jax.experimental.pallas import tpu as pltpu

F32 = jnp.float32
BF16 = jnp.bfloat16

GRID_W = 64
ROPE_THETA = 10000.0
NORM_EPS = 1e-6

A_HEADS = 8
A_KV_HEADS = 2
HEAD_DIM = 64
A_WIDTH = A_HEADS * HEAD_DIM
A_KV_WIDTH = A_KV_HEADS * HEAD_DIM
B_HEADS = 8
B_WIDTH = B_HEADS * HEAD_DIM
LORA = 64
LNX_EPS = 64e-5
C_HEADS = 4
GN_EPS = 1e-5

LANES = 128
MXU_DIM = 256
RWKV_CHUNK = 64
RWKV_GROUP = 4
RET_CHUNK = 256
VMEM_LIMIT = 56 * 1024 * 1024


def _cparams(sem):
    return pltpu.CompilerParams(dimension_semantics=sem, vmem_limit_bytes=VMEM_LIMIT)


def _dot(a, b):
    return jnp.dot(a, b, preferred_element_type=F32)


def _dot_nt(a, b):
    return lax.dot_general(a, b, (((1,), (1,)), ((), ())), preferred_element_type=F32)


def _dot_tn(a, b):
    return lax.dot_general(a, b, (((0,), (0,)), ((), ())), preferred_element_type=F32)


def _bmm(a, b):
    return lax.dot_general(a, b, (((2,), (1,)), ((0,), (0,))), preferred_element_type=F32)


def _bmm_nt(a, b):
    return lax.dot_general(a, b, (((2,), (2,)), ((0,), (0,))), preferred_element_type=F32)


def _bmm_tn(a, b):
    return lax.dot_general(a, b, (((1,), (1,)), ((0,), (0,))), preferred_element_type=F32)


def _sigmoid(x):
    return 1.0 / (1.0 + jnp.exp(-x))


def _silu(x):
    return x * _sigmoid(x)


def _rope_tables(T, dim):
    half = dim // 2
    t = np.arange(T)
    row = (t // GRID_W).astype(np.float32)
    col = (t % GRID_W).astype(np.float32)
    inv_freq = (ROPE_THETA ** (-np.arange(0, half, 2, dtype=np.float32) / half)).astype(np.float32)
    ar = row[:, None] * inv_freq
    ac = col[:, None] * inv_freq
    cos = np.concatenate([np.cos(ar), np.cos(ar), np.cos(ac), np.cos(ac)], axis=1)
    sin = np.concatenate([-np.sin(ar), np.sin(ar), -np.sin(ac), np.sin(ac)], axis=1)
    return jnp.asarray(cos, F32), jnp.asarray(sin, F32)


def _head_norm_rope(x, gain, cos, sin):
    lane = lax.broadcasted_iota(jnp.int32, (1, LANES), 1)
    head0 = lane < HEAD_DIM
    sq = x * x
    s0 = jnp.sum(jnp.where(head0, sq, 0.0), axis=-1, keepdims=True)
    s1 = jnp.sum(jnp.where(head0, 0.0, sq), axis=-1, keepdims=True)
    ms = jnp.where(head0, s0, s1) * (1.0 / HEAD_DIM)
    y = x * lax.rsqrt(ms + NORM_EPS) * gain
    q4 = HEAD_DIM // 4
    first = (lane % (2 * q4)) < q4
    partner = jnp.where(first, pltpu.roll(y, LANES - q4, 1), pltpu.roll(y, q4, 1))
    return y * cos + partner * sin


def _even_col_kind(c):
    a_gate = A_WIDTH + 2 * A_KV_WIDTH
    b_gate = a_gate + A_WIDTH + 3 * B_WIDTH + 4 * LORA
    if c < A_WIDTH:
        return "q"
    if c < A_WIDTH + A_KV_WIDTH:
        return "k"
    if a_gate <= c < a_gate + A_WIDTH or c >= b_gate:
        return "silu"
    return "plain"


def _odd_col_kind(c, n_out):
    if c < n_out // 6:
        return "rope_q"
    if c < n_out // 3:
        return "rope_k"
    if c >= 2 * n_out // 3:
        return "silu"
    return "plain"


def _proj_kernel(x_ref, *refs, **static):
    _proj_body(x_ref[...], *refs, **static)


def _proj_body(x, g_ref, w_ref, cos_ref, sin_ref, *rest, col_chunk, even, k_scale):
    if even:
        qg_ref, kg_ref, o_ref = rest
    else:
        (o_ref,) = rest
    ms = jnp.mean(x * x, axis=-1, keepdims=True)
    hn = (x * lax.rsqrt(ms + NORM_EPS) * g_ref[...]).astype(BF16)
    n_out = o_ref.shape[-1]
    q_scale = HEAD_DIM ** -0.5 * np.log2(np.e)
    for c0 in range(0, n_out, col_chunk):
        acc = _dot(hn, w_ref[:, c0:c0 + col_chunk])
        for b0 in range(0, col_chunk, LANES):
            c = c0 + b0
            xb = acc[:, b0:b0 + LANES]
            kind = _even_col_kind(c) if even else _odd_col_kind(c, n_out)
            if kind == "q":
                yb = _head_norm_rope(xb, qg_ref[...], cos_ref[...], sin_ref[...]) * q_scale
            elif kind == "k":
                yb = _head_norm_rope(xb, kg_ref[...], cos_ref[...], sin_ref[...])
            elif kind in ("rope_q", "rope_k"):
                t0 = c % (2 * LANES)
                yb = xb * cos_ref[:, t0:t0 + LANES] + pltpu.roll(xb, LANES // 2, 1) * sin_ref[:, t0:t0 + LANES]
                if kind == "rope_k":
                    yb = yb * k_scale
            elif kind == "silu":
                yb = _silu(xb)
            else:
                yb = xb
            o_ref[:, c:c + LANES] = yb.astype(o_ref.dtype)


def _resident(shape):
    return pl.BlockSpec(shape, lambda i: (0,) * len(shape), pipeline_mode=pl.Buffered(1))


def _proj_operands(gain, w_bf, cos, sin, *, tm, seq, head_gains):
    d, n_out = w_bf.shape
    tiles_per_seq = seq // tm
    table = pl.BlockSpec((tm, cos.shape[1]), lambda i: (i % tiles_per_seq, 0))
    in_specs = [_resident((1, d)), _resident((d, n_out)), table, table]
    args = [gain.reshape(1, d), w_bf, cos, sin]
    if head_gains is not None:
        in_specs += [_resident((1, LANES))] * 2
        args += [jnp.tile(g, LANES // HEAD_DIM).reshape(1, LANES) for g in head_gains]
    return in_specs, args


def _proj(h2, operands, static, *, tm):
    n_tok, d = h2.shape
    in_specs, args = operands
    n_out = args[1].shape[1]
    return pl.pallas_call(
        functools.partial(_proj_kernel, **static),
        grid=(n_tok // tm,),
        in_specs=[pl.BlockSpec((tm, d), lambda i: (i, 0))] + in_specs,
        out_specs=pl.BlockSpec((tm, n_out), lambda i: (i, 0)),
        out_shape=jax.ShapeDtypeStruct((n_tok, n_out), BF16),
        compiler_params=_cparams(("parallel",)),
        name="proj",
    )(h2, *args)


def _out_kernel(*refs, n_in, nxt):
    ins = refs[:n_in]
    w_ref, g_ref, h_ref = refs[n_in:n_in + 3]
    acc = None
    r0 = 0
    for m_ref in ins:
        k = m_ref.shape[-1]
        part = _dot(m_ref[...], w_ref[r0:r0 + k, :])
        acc = part if acc is None else acc + part
        r0 += k
    ms = jnp.mean(acc * acc, axis=-1, keepdims=True)
    h_new = h_ref[...] + acc * lax.rsqrt(ms + NORM_EPS) * g_ref[...]
    if nxt is None:
        (o_ref,) = refs[n_in + 3:]
    else:
        *proj_refs, o_ref, p_ref = refs[n_in + 3:]
        _proj_body(h_new, *proj_refs, p_ref, **nxt)
    o_ref[...] = h_new


def _out_proj(mixed, w_bf, gain, h2, *, tm, nxt=None):
    n_tok, d = h2.shape
    k_total = w_bf.shape[0]
    in_specs = [pl.BlockSpec((tm, m.shape[1]), lambda i: (i, 0)) for m in mixed]
    in_specs += [_resident((k_total, d)), _resident((1, d)), pl.BlockSpec((tm, d), lambda i: (i, 0))]
    args = [*mixed, w_bf, gain.reshape(1, d), h2]
    out_specs = pl.BlockSpec((tm, d), lambda i: (i, 0))
    out_shape = jax.ShapeDtypeStruct((n_tok, d), F32)
    static = None
    if nxt is not None:
        (nxt_specs, nxt_args), static = nxt
        n_next = nxt_args[1].shape[1]
        in_specs += nxt_specs
        args += nxt_args
        out_specs = [out_specs, pl.BlockSpec((tm, n_next), lambda i: (i, 0))]
        out_shape = [out_shape, jax.ShapeDtypeStruct((n_tok, n_next), BF16)]
    return pl.pallas_call(
        functools.partial(_out_kernel, n_in=len(mixed), nxt=static),
        grid=(n_tok // tm,),
        in_specs=in_specs,
        out_specs=out_specs,
        out_shape=out_shape,
        compiler_params=_cparams(("parallel",)),
        name="out_proj",
    )(*args)


ONES_ROWS = 16


def _attn_kernel(q_ref, kv_ref, ga_ref, gb_ref, o_ref, vt_ref, *, tq):
    t_idx = pl.program_id(1)
    T = kv_ref.shape[1]
    lane = lax.broadcasted_iota(jnp.int32, (1, LANES), 1)
    head0 = lane < HEAD_DIM

    @pl.when(t_idx == 0)
    def _():
        vt = kv_ref[0, :, LANES:2 * LANES].astype(F32).T
        for g in range(A_KV_HEADS):
            vt_ref[g, 0:HEAD_DIM, :] = vt[g * HEAD_DIM:(g + 1) * HEAD_DIM].astype(BF16)
            vt_ref[g, HEAD_DIM:HEAD_DIM + ONES_ROWS, :] = jnp.ones((ONES_ROWS, T), BF16)

    row0 = pl.multiple_of(t_idx * tq, tq)
    kn = kv_ref[0, :, 0:LANES]
    rep = A_HEADS // A_KV_HEADS

    def scores(head):
        pair, sub, g = head // 2, head % 2, head // rep
        qn = q_ref[0, pl.ds(row0, tq), pair * LANES:(pair + 1) * LANES].astype(F32)
        src = qn if sub == g else pltpu.roll(qn, HEAD_DIM, 1)
        grp = head0 if g == 0 else jnp.logical_not(head0)
        return _dot_nt(kn, jnp.where(grp, src, 0.0).astype(BF16))

    def values(head, st):
        p = jnp.exp2(st - jnp.max(st, axis=0, keepdims=True)).astype(BF16)
        ox = _dot(vt_ref[head // rep], p)
        return ox[0:HEAD_DIM] * (1.0 / ox[HEAD_DIM:HEAD_DIM + 1])

    outs = []
    st = scores(0)
    for head in range(A_HEADS):
        st_next = scores(head + 1) if head + 1 < A_HEADS else None
        outs.append(values(head, st))
        st = st_next
        if head % 2 == 1:
            pair = head // 2
            o_pair = jnp.concatenate(outs, axis=0).T
            outs = []
            gref = ga_ref if pair < 2 else gb_ref
            gc = (pair % 2) * LANES
            gate = gref[0, pl.ds(row0, tq), gc:gc + LANES].astype(F32)
            o_ref[0, :, pair * LANES:(pair + 1) * LANES] = (o_pair * gate).astype(o_ref.dtype)


def _attention(proj3, *, tq):
    B, T, _ = proj3.shape
    return pl.pallas_call(
        functools.partial(_attn_kernel, tq=tq),
        grid=(B, T // tq),
        in_specs=[
            pl.BlockSpec((1, T, A_WIDTH), lambda b, t: (b, 0, 0)),
            pl.BlockSpec((1, T, 2 * A_KV_WIDTH), lambda b, t: (b, 0, A_WIDTH // (2 * A_KV_WIDTH))),
            pl.BlockSpec((1, T, 2 * LANES), lambda b, t: (b, 0, 3)),
            pl.BlockSpec((1, T, 2 * LANES), lambda b, t: (b, 0, 4)),
        ],
        out_specs=pl.BlockSpec((1, tq, A_WIDTH), lambda b, t: (b, t, 0)),
        out_shape=jax.ShapeDtypeStruct((B, T, A_WIDTH), BF16),
        scratch_shapes=[pltpu.VMEM((A_KV_HEADS, HEAD_DIM + ONES_ROWS, T), BF16)],
        compiler_params=_cparams(("parallel", "arbitrary")),
        name="attention",
    )(proj3, proj3, proj3, proj3)


RWKV_LANES = MXU_DIM


def _rwkv_groups(T):
    n_chunks = T // RWKV_CHUNK
    cg = min(RWKV_GROUP, n_chunks // 2)
    n_groups = n_chunks // cg
    assert n_groups % 2 == 0 and n_groups * cg == n_chunks
    return n_chunks, cg, n_groups


def _rwkv_kernel(r_ref, k_ref, v_ref, lo_ref, gate_ref,
                 mur_ref, muk_ref, muv_ref, mulo_ref, wbig_ref,
                 w0f_ref, a0f_ref, w0b_ref, a0b_ref,
                 kk_ref, ka_ref, rk_ref, lng_ref, lnb_ref,
                 o_ref,
                 r_s, v_s, a_s, k_s, b_s, lw_s, y_s, st_s, ss_s,
                 mt_a, gt_a, rp_a, y1_a, dec_a, mt_b, gt_b, rp_b, y1_b, dec_b):
    T = r_ref.shape[1]
    L = RWKV_CHUNK
    HW = RWKV_LANES
    assert L == HEAD_DIM
    n_chunks, CG, n_groups = _rwkv_groups(T)
    GT = CG * L
    lane = lax.broadcasted_iota(jnp.int32, (1, LANES), 1)
    head0 = lane < HEAD_DIM

    SHIFT_ROWS = LANES
    n_shift = T // SHIFT_ROWS
    win_start = [min(max(i * SHIFT_ROWS - (MXU_DIM - SHIFT_ROWS) // 2, 0), T - MXU_DIM) for i in range(n_shift)]
    t_in = lax.broadcasted_iota(jnp.int32, (SHIFT_ROWS, MXU_DIM), 0)
    j_in = lax.broadcasted_iota(jnp.int32, (SHIFT_ROWS, MXU_DIM), 1)
    pick = {off: jnp.where((j_in == t_in + off - 1) | (j_in == t_in + off + 1), 1.0, 0.0).astype(BF16)
            for off in sorted({i * SHIFT_ROWS - s for i, s in enumerate(win_start)})}

    def shift(ref, mu_ref):
        u = ref[0].astype(F32)
        both = jnp.concatenate(
            [_dot(pick[i * SHIFT_ROWS - s], ref[0, s:s + MXU_DIM, :]) for i, s in enumerate(win_start)], axis=0)
        return u + mu_ref[...] * (0.5 * both - u)

    def head_sum(x):
        parts = []
        for c0 in range(0, HW, LANES):
            xc = x[:, c0:c0 + LANES]
            s0 = jnp.sum(jnp.where(head0, xc, 0.0), axis=-1, keepdims=True)
            s1 = jnp.sum(jnp.where(head0, 0.0, xc), axis=-1, keepdims=True)
            parts.append(jnp.where(head0, s0, s1))
        return jnp.concatenate(parts, axis=1)

    r = shift(r_ref, mur_ref)
    kb = shift(k_ref, muk_ref)
    vb = shift(v_ref, muv_ref)
    lo = shift(lo_ref, mulo_ref)
    lo_t = jnp.concatenate([jnp.tanh(lo[:, :2 * LORA]), lo[:, 2 * LORA:]], axis=1).astype(BF16)
    z = _dot(lo_t, wbig_ref[0])

    kkv = kb * kk_ref[...]
    kkn = kkv * lax.rsqrt(jnp.maximum(head_sum(kkv * kkv), 1e-24))
    bonus = head_sum(r * kb * rk_ref[...]) * vb

    r_s[...] = r.astype(BF16)
    v_s[...] = vb.astype(BF16)
    a_s[...] = (-kkn).astype(BF16)
    for d, (w0_ref, a0_ref) in enumerate(((w0f_ref, a0f_ref), (w0b_ref, a0b_ref))):
        xw = w0_ref[...] + z[:, d * HW:(d + 1) * HW]
        lw_s[d] = (-np.exp(-0.5)) * _sigmoid(xw)
        iclr = _sigmoid(a0_ref[...] + z[:, (2 + d) * HW:(3 + d) * HW])
        k_s[d] = (kb * (1.0 + (iclr - 1.0) * ka_ref[...])).astype(BF16)
        b_s[d] = (kkn * iclr).astype(BF16)

    pos = lax.broadcasted_iota(jnp.int32, (GT, 1), 0) % L
    trow = lax.broadcasted_iota(jnp.int32, (L, HW), 0)
    scol = lax.broadcasted_iota(jnp.int32, (L, HW), 1) % L
    eye = (trow == scol).astype(F32)
    blk_r = lax.broadcasted_iota(jnp.int32, (HW, HW), 0) // L
    blk_c = lax.broadcasted_iota(jnp.int32, (HW, HW), 1) // HEAD_DIM
    bdm = jnp.where(blk_r == blk_c, 1.0, 0.0).astype(BF16)

    def bd(x):
        return jnp.concatenate([x] * (HW // L), axis=1) * bdm

    def operands(d, gi):
        reverse = d == 1
        sl = pl.ds(pl.multiple_of(gi * GT, GT), GT)
        lw = lw_s[d, sl, :]
        cf = lw
        sh = 1
        while sh < L:
            cf = cf + jnp.where(pos >= sh, pltpu.roll(cf, sh, 0), 0.0)
            sh *= 2
        to3 = lambda x: x.reshape(CG, L, HW)
        cf3, lw3 = to3(cf), to3(lw)
        tot = cf3[:, L - 1:L, :]
        if reverse:
            cin = tot - cf3 + lw3
            cex = tot - cf3
        else:
            cin = cf3
            cex = cf3 - lw3
        a3, r3 = to3(a_s[sl, :].astype(F32)), to3(r_s[sl, :].astype(F32))
        k3, b3 = to3(k_s[d, sl, :].astype(F32)), to3(b_s[d, sl, :].astype(F32))
        v_bf = to3(v_s[sl, :])
        ek = jnp.exp(-cin)
        ekt = jnp.exp(tot - cin)
        return dict(
            at=(a3 * jnp.exp(cex)).astype(BF16), rh=r3 * jnp.exp(cin),
            bh=(b3 * ek).astype(BF16), kh=(k3 * ek).astype(BF16),
            bt=(b3 * ekt).astype(BF16), kt=(k3 * ekt).astype(BF16),
            v=v_bf, dec=jnp.exp(tot))

    def tri(x, fwd_mask, bwd_mask):
        return jnp.concatenate([jnp.where(fwd_mask, x[:CG], 0.0), jnp.where(bwd_mask, x[CG:], 0.0)], axis=0)

    def operands2(gf, gb):
        of, ob = operands(0, gf), operands(1, gb)
        return {name: jnp.concatenate([of[name], ob[name]], axis=0) for name in of}

    def precompute(ops, mt_o, gt_o, rp_o, y1_o, dec_o):
        cat = lambda name: ops[name]
        at, rh, bt, kt, v_bf = cat("at"), cat("rh"), cat("bt"), cat("kt"), cat("v")
        vs = bd(v_bf)
        strict = lambda x: tri(x, trow > scol, trow < scol)
        incl = lambda x: tri(x, trow >= scol, trow <= scol)

        lhs = jnp.concatenate([at, rh.astype(BF16)], axis=1)
        sb = _bmm_nt(lhs, bd(cat("bh")))
        sk = _bmm_nt(lhs, bd(cat("kh")))
        aab = strict(sb[:, :L])
        arb = incl(sb[:, L:]).astype(BF16)
        aak = strict(sk[:, :L]).astype(BF16)
        ark = incl(sk[:, L:]).astype(BF16)

        tinv = eye + aab
        aab_bf = aab.astype(BF16)
        pw = _bmm(aab_bf, bd(aab_bf))
        step = 4
        while step < L:
            both = _bmm(jnp.concatenate([tinv.astype(BF16), pw.astype(BF16)], axis=1), bd(pw.astype(BF16)))
            tinv = tinv + both[:, :L]
            pw = both[:, L:]
            step *= 2
        tinv = (tinv + _bmm(tinv.astype(BF16), bd(pw.astype(BF16)))).astype(BF16)

        akv = _bmm(jnp.concatenate([aak, ark], axis=1), vs)
        ap = _bmm(tinv, bd(at)).astype(BF16)
        u0 = _bmm(tinv, bd(akv[:, :L].astype(BF16))).astype(BF16)
        split = lambda x: x.reshape((2, CG) + x.shape[1:])
        rp_o[...] = split((rh + _bmm(arb, bd(ap))).astype(BF16))
        y1_o[...] = split(_bmm(arb, bd(u0)) + akv[:, L:])
        mt_o[...] = split(_bmm_tn(ap, bt).astype(BF16))
        gt_o[...] = split(_bmm_tn(jnp.concatenate([u0, v_bf], axis=1),
                                  jnp.concatenate([bt, kt], axis=1)))
        dec_o[...] = split(cat("dec"))

    def recur_readout(gf, gb, mt_i, gt_i, rp_i, y1_i, dec_i):
        for i in range(CG):
            for d, j in ((0, i), (1, CG - 1 - i)):
                s = st_s[d]
                s_bf = s.astype(BF16) * bdm
                ss_s[d, j] = s_bf
                st_s[d] = s * dec_i[d, j] + _dot(s_bf, mt_i[d, j]) + gt_i[d, j]
        for d, g in ((0, gf), (1, gb)):
            y = _bmm_nt(rp_i[d], ss_s[d]) + y1_i[d]
            rows = pl.ds(pl.multiple_of((g + 1) * GT, GT), GT)
            y_s[rows, :] += y.reshape(GT, HW)

    buf_a = (mt_a, gt_a, rp_a, y1_a, dec_a)
    buf_b = (mt_b, gt_b, rp_b, y1_b, dec_b)
    for ref in buf_b + (st_s, y_s):
        ref[...] = jnp.zeros_like(ref)

    def pipelined(j, carry):
        g = 2 * j
        precompute(operands2(g, n_groups - 1 - g), *buf_a)
        recur_readout(g - 1, n_groups - g, *buf_b)
        precompute(operands2(g + 1, n_groups - 2 - g), *buf_b)
        recur_readout(g, n_groups - 1 - g, *buf_a)
        return carry

    lax.fori_loop(0, n_groups // 2, pipelined, 0)
    recur_readout(n_groups - 1, 0, *buf_b)

    y = y_s[GT:GT + T, :]
    mean = head_sum(y) * (1.0 / HEAD_DIM)
    yc = y - mean
    var = head_sum(yc * yc) * (1.0 / HEAD_DIM)
    yn = yc * lax.rsqrt(var + LNX_EPS) * lng_ref[...] + lnb_ref[...]
    o_ref[0] = ((yn + bonus) * gate_ref[0].astype(F32)).astype(o_ref.dtype)


def _rwkv(proj3, p):
    B, T, _ = proj3.shape
    HW = RWKV_LANES
    L = RWKV_CHUNK
    n_q = B_WIDTH // HW
    base = (A_WIDTH + 2 * A_KV_WIDTH + A_WIDTH) // HW
    _, cg, n_groups = _rwkv_groups(T)
    vec = lambda a: a.reshape(1, -1)
    blk = lambda off: pl.BlockSpec((1, T, HW), lambda b, h: (b, 0, off + h))
    pvec = lambda off=0: pl.BlockSpec((1, HW), lambda b, h: (0, off + h))
    mu = vec(p["mu"])
    in_specs = [
        blk(base), blk(base + n_q), blk(base + 2 * n_q),
        pl.BlockSpec((1, T, 4 * LORA), lambda b, h: (b, 0, base + 3 * n_q)),
        blk(base + 3 * n_q + 1),
        pvec(0), pvec(n_q), pvec(2 * n_q),
        pl.BlockSpec((1, 4 * LORA), lambda b, h: (0, 3 * n_q)),
        pl.BlockSpec((1, 4 * LORA, 4 * HW), lambda b, h: (h, 0, 0)),
    ] + [pvec()] * 9
    group_bufs = [
        pltpu.VMEM((2, cg, HW, HW), BF16),
        pltpu.VMEM((2, cg, HW, HW), F32),
        pltpu.VMEM((2, cg, L, HW), BF16),
        pltpu.VMEM((2, cg, L, HW), F32),
        pltpu.VMEM((2, cg, 1, HW), F32),
    ]
    scratch = [
        pltpu.VMEM((T, HW), BF16),
        pltpu.VMEM((T, HW), BF16),
        pltpu.VMEM((T, HW), BF16),
        pltpu.VMEM((2, T, HW), BF16),
        pltpu.VMEM((2, T, HW), BF16),
        pltpu.VMEM((2, T, HW), F32),
        pltpu.VMEM(((n_groups + 2) * cg * L, HW), F32),
        pltpu.VMEM((2, HW, HW), F32),
        pltpu.VMEM((2, cg, HW, HW), BF16),
    ] + group_bufs + group_bufs
    return pl.pallas_call(
        _rwkv_kernel,
        grid=(B, n_q),
        in_specs=in_specs,
        out_specs=pl.BlockSpec((1, T, HW), lambda b, h: (b, 0, h)),
        out_shape=jax.ShapeDtypeStruct((B, T, B_WIDTH), BF16),
        scratch_shapes=scratch,
        compiler_params=_cparams(("parallel", "parallel")),
        name="rwkv7",
    )(proj3, proj3, proj3, proj3, proj3, mu, mu, mu, mu, p["wbig"],
      vec(p["w0_f"]), vec(p["a0_f"]), vec(p["w0_b"]), vec(p["a0_b"]),
      vec(p["k_k"]), vec(p["k_a"]), vec(p["r_k"]), vec(p["lnx_g"]), vec(p["lnx_b"]))


def _lora_weights(w2_f, w2_b, a2_f, a2_b):
    HW = RWKV_LANES
    n_q = B_WIDTH // HW
    out = jnp.zeros((n_q, 4 * LORA, 4 * HW), F32)
    for i, w in enumerate((w2_f, w2_b, a2_f, a2_b)):
        wp = w.reshape(LORA, n_q, HW).transpose(1, 0, 2)
        out = out.at[:, i * LORA:(i + 1) * LORA, i * HW:(i + 1) * HW].set(wp)
    return out.astype(BF16)


RET_HEADS_PER_STEP = 2


def _ret_kernel(lg_ref, q_ref, k_ref, v_ref, g_ref, gn_ref, o_ref, st_s, sb_s):
    HP = RET_HEADS_PER_STEP
    T = q_ref.shape[1]
    C = min(RET_CHUNK, T)
    n = T // C
    dk = q_ref.shape[-1] // HP
    dv = o_ref.shape[-1] // HP
    idx = lax.broadcasted_iota(jnp.int32, (C, 1), 0).astype(F32)
    ji = lax.broadcasted_iota(jnp.int32, (C, C), 0)
    li = lax.broadcasted_iota(jnp.int32, (C, C), 1)
    diff = (ji - li).astype(F32)
    wide = lambda col: jnp.broadcast_to(col, (C, dk)).astype(BF16)

    def tables(hh):
        h = pl.program_id(1) * HP + hh
        lgf, lgb = lg_ref[h, 0], lg_ref[h, 1]
        return dict(
            dmat=jnp.exp(jnp.where(ji >= li, diff * lgf, -diff * lgb)),
            qdec_f=wide(jnp.exp((idx + 1.0) * lgf)), kdec_f=wide(jnp.exp((C - 1.0 - idx) * lgf)),
            qdec_b=wide(jnp.exp((C - 1.0 - idx) * lgb)), kdec_b=wide(jnp.exp((idx + 1.0) * lgb)),
            cdec_f=jnp.exp(jnp.full((1, 1), float(C), F32) * lgf),
            cdec_b=jnp.exp(jnp.full((1, 1), float(C), F32) * lgb))

    tabs = [tables(hh) for hh in range(HP)]

    def chunk(hh, i):
        sl = pl.ds(i * C, C)
        ks, vs = slice(hh * dk, (hh + 1) * dk), slice(hh * dv, (hh + 1) * dv)
        return sl, vs, q_ref[0, sl, ks], k_ref[0, sl, ks], v_ref[0, sl, vs]

    st_s[...] = jnp.zeros_like(st_s)
    for c in range(n - 1, -1, -1):
        for hh, t in enumerate(tabs):
            st = st_s[hh]
            sb_s[hh, c] = st.astype(BF16)
            if c > 0:
                _, _, _, k, v = chunk(hh, c)
                st_s[hh] = st * t["cdec_b"] + _dot_tn(k * t["kdec_b"], v)

    st_s[...] = jnp.zeros_like(st_s)
    inv_n = 1.0 / dv
    for c in range(n):
        for hh, t in enumerate(tabs):
            sl, vs, q, k, v = chunk(hh, c)
            st = st_s[hh]
            s = (_dot_nt(q, k) * t["dmat"]).astype(BF16)
            lhs = jnp.concatenate([s, q * t["qdec_f"], q * t["qdec_b"]], axis=1)
            rhs = jnp.concatenate([v, st.astype(BF16), sb_s[hh, c]], axis=0)
            o = _dot(lhs, rhs)
            if c < n - 1:
                st_s[hh] = st * t["cdec_f"] + _dot_tn(k * t["kdec_f"], v)
            mean = jnp.sum(o, axis=-1, keepdims=True) * inv_n
            oc = o - mean
            var = jnp.sum(oc * oc, axis=-1, keepdims=True) * inv_n
            y = oc * lax.rsqrt(var + GN_EPS) * gn_ref[:, vs]
            o_ref[0, sl, vs] = (g_ref[0, sl, vs].astype(F32) * y).astype(o_ref.dtype)


def _retention(proj3, gn_g):
    B, T, width = proj3.shape
    dk = (width // 6) // C_HEADS
    dv = 2 * dk
    lg_f = np.log(1.0 - 2.0 ** (-5.0 - np.arange(C_HEADS, dtype=np.float32))).astype(np.float32)
    lg = jnp.asarray(np.stack([lg_f, lg_f[::-1]], axis=1), F32)
    hp = RET_HEADS_PER_STEP
    steps = C_HEADS // hp
    return pl.pallas_call(
        _ret_kernel,
        grid=(B, steps),
        in_specs=[
            pl.BlockSpec(memory_space=pltpu.SMEM),
            pl.BlockSpec((1, T, hp * dk), lambda b, h: (b, 0, h)),
            pl.BlockSpec((1, T, hp * dk), lambda b, h: (b, 0, steps + h)),
            pl.BlockSpec((1, T, hp * dv), lambda b, h: (b, 0, steps + h)),
            pl.BlockSpec((1, T, hp * dv), lambda b, h: (b, 0, 2 * steps + h)),
            pl.BlockSpec((1, hp * dv), lambda b, h: (0, h)),
        ],
        out_specs=pl.BlockSpec((1, T, hp * dv), lambda b, h: (b, 0, h)),
        out_shape=jax.ShapeDtypeStruct((B, T, C_HEADS * dv), BF16),
        scratch_shapes=[pltpu.VMEM((hp, dk, dv), F32),
                        pltpu.VMEM((hp, T // min(RET_CHUNK, T), dk, dv), BF16)],
        compiler_params=_cparams(("parallel", "parallel")),
        name="retention",
    )(lg, proj3, proj3, proj3, proj3, gn_g.reshape(1, -1))


def kernel(x, pre_gain, post_gain, even_w_in, even_mu, even_q_gain, even_k_gain, even_k_k, even_k_a, even_r_k, even_w0_f, even_w2_f, even_a0_f, even_a2_f, even_w0_b, even_w2_b, even_a0_b, even_a2_b, even_lnx_g, even_lnx_b, even_w_out, odd_w_in, odd_gn_g, odd_w_out):
    B, T, D = x.shape
    n_tok = B * T
    depth = pre_gain.shape[0]
    tm = min(512, T)
    tq = min(1024, T)
    cos_a, sin_a = _rope_tables(T, HEAD_DIM)
    cos_a, sin_a = jnp.tile(cos_a, (1, 2)), jnp.tile(sin_a, (1, 2))
    dk = odd_w_in.shape[-1] // 6 // C_HEADS
    cos_c, sin_c = _rope_tables(T, dk)

    def projection(layer):
        i = layer // 2
        if layer % 2 == 0:
            operands = _proj_operands(pre_gain[layer], even_w_in[i].astype(BF16), cos_a, sin_a, tm=tm, seq=T,
                                      head_gains=(even_q_gain[i], even_k_gain[i]))
            return operands, dict(col_chunk=MXU_DIM, even=True, k_scale=1.0)
        operands = _proj_operands(pre_gain[layer], odd_w_in[i].astype(BF16), cos_c, sin_c, tm=tm, seq=T,
                                  head_gains=None)
        return operands, dict(col_chunk=1024, even=False, k_scale=dk ** -0.5)

    def finish_layer(layer, mixed, w_out, h):
        w_bf = w_out.astype(BF16)
        if layer + 1 == depth:
            return _out_proj(mixed, w_bf, post_gain[layer], h, tm=tm), None
        return _out_proj(mixed, w_bf, post_gain[layer], h, tm=tm, nxt=projection(layer + 1))

    h = x.reshape(n_tok, D)
    proj = _proj(h, *projection(0), tm=tm)
    for layer in range(depth):
        i = layer // 2
        if layer % 2 == 0:
            proj3 = proj.reshape(B, T, -1)
            out_a = _attention(proj3, tq=tq)
            params = dict(
                mu=even_mu[i], k_k=even_k_k[i], k_a=even_k_a[i], r_k=even_r_k[i],
                w0_f=even_w0_f[i], a0_f=even_a0_f[i], w0_b=even_w0_b[i], a0_b=even_a0_b[i],
                lnx_g=even_lnx_g[i], lnx_b=even_lnx_b[i],
                wbig=_lora_weights(even_w2_f[i], even_w2_b[i], even_a2_f[i], even_a2_b[i]),
            )
            out_b = _rwkv(proj3, params)
            mixed = [out_a.reshape(n_tok, -1), out_b.reshape(n_tok, -1)]
            h, proj = finish_layer(layer, mixed, even_w_out[i], h)
        else:
            ret = _retention(proj.reshape(B, T, -1), odd_gn_g[i])
            h, proj = finish_layer(layer, [ret.reshape(n_tok, -1)], odd_w_out[i], h)
    return h.reshape(B, T, D)
```
